```python
import math
import jax
import jax.numpy as jnp
from jax import lax
import numpy as np

D_MODEL = 1024
BATCH = 8
SEQ = 4096
DEPTH = 4

GRID_W = 64
EPS = 1e-6

S5_WIDTH = D_MODEL // 2
S5_GROUP = 16
S5_GROUPS = S5_WIDTH // S5_GROUP
S5_STATE = 64
S5_MAX_RE = -1e-4

HEAD_DIM = 64
N_Q_HEADS = (D_MODEL // 2) // HEAD_DIM
N_KV_HEADS = 2
Q_PER_KV = N_Q_HEADS // N_KV_HEADS
ATTN_WIDTH = N_Q_HEADS * HEAD_DIM
KV_WIDTH = N_KV_HEADS * HEAD_DIM
ROPE_AXIS_DIM = HEAD_DIM // 2
ROPE_FREQS = ROPE_AXIS_DIM // 2
ROPE_BASE = 10000.0
Q_BLOCK = 128

HYB_IN = S5_WIDTH + ATTN_WIDTH + 2 * KV_WIDTH
HYB_OUT = S5_WIDTH + ATTN_WIDTH

LRU_WIDTH = D_MODEL
LRU_HEADS = 16
LRU_BLOCK = LRU_WIDTH // LRU_HEADS
CONV_WIDTH = 4
CONV_LEFT = 2
LRU_C = 8.0

D_FF = 4 * D_MODEL

N_EVEN = (DEPTH + 1) // 2
N_ODD = DEPTH // 2

kernel_name = "hybrid_s5_gqa_rglru_encoder"


def rms_norm(x, w):
    xf = x.astype(jnp.float32)
    y = xf * lax.rsqrt(jnp.mean(xf * xf, axis=-1, keepdims=True) + EPS)
    return (y * w.astype(jnp.float32)).astype(x.dtype)


def _linear_combine(e1, e2):
    a1, b1 = e1
    a2, b2 = e2
    return a1 * a2, a2 * b1 + b2


def _complex_linear_combine(e1, e2):
    ar1, ai1, br1, bi1 = e1
    ar2, ai2, br2, bi2 = e2
    return (ar1 * ar2 - ai1 * ai2,
            ar1 * ai2 + ai1 * ar2,
            ar2 * br1 - ai2 * bi1 + br2,
            ar2 * bi1 + ai2 * br1 + bi2)


def s5_mixer(u, lam_re, lam_im, log_dt, b_re, b_im, c_re, c_im, d, glu_w, glu_b):
    bsz, seq, _ = u.shape
    f32 = jnp.float32
    uf = u.astype(f32).reshape(bsz, seq, S5_GROUPS, S5_GROUP)
    y = uf * d.astype(f32).reshape(S5_GROUPS, S5_GROUP)
    for direction in range(2):
        lr = jnp.minimum(lam_re[direction].astype(f32), S5_MAX_RE)
        li = lam_im[direction].astype(f32)
        dt = jnp.exp(log_dt[direction].astype(f32))[:, None]
        mag = jnp.exp(lr * dt)
        abar_re = mag * jnp.cos(li * dt)
        abar_im = mag * jnp.sin(li * dt)
        den = lr * lr + li * li
        nr = abar_re - 1.0
        f_re = (nr * lr + abar_im * li) / den
        f_im = (abar_im * lr - nr * li) / den
        br = b_re[direction].astype(f32)
        bi = b_im[direction].astype(f32)
        bb_re = f_re[..., None] * br - f_im[..., None] * bi
        bb_im = f_re[..., None] * bi + f_im[..., None] * br
        bu_re = jnp.einsum("bsgc,gpc->bsgp", uf, bb_re)
        bu_im = jnp.einsum("bsgc,gpc->bsgp", uf, bb_im)
        a_shape = (1, seq, S5_GROUPS, S5_STATE)
        _, _, h_re, h_im = lax.associative_scan(
            _complex_linear_combine,
            (jnp.broadcast_to(abar_re, a_shape), jnp.broadcast_to(abar_im, a_shape), bu_re, bu_im),
            reverse=(direction == 1), axis=1)
        y = y + (jnp.einsum("bsgp,gcp->bsgc", h_re, c_re[direction].astype(f32))
                 - jnp.einsum("bsgp,gcp->bsgc", h_im, c_im[direction].astype(f32)))
    y = y.reshape(bsz, seq, S5_WIDTH)
    g = jax.nn.gelu(y)
    out = g * jax.nn.sigmoid(g @ glu_w.astype(f32) + glu_b.astype(f32))
    return out.astype(u.dtype)


def axial_rope_tables(seq):
    rows = seq // GRID_W
    row_idx = jnp.repeat(jnp.arange(rows, dtype=jnp.float32), GRID_W)
    col_idx = jnp.tile(jnp.arange(GRID_W, dtype=jnp.float32), rows)
    inv_freq = ROPE_BASE ** (-jnp.arange(ROPE_FREQS, dtype=jnp.float32) / ROPE_FREQS)
    ang = jnp.stack([row_idx[:, None] * inv_freq, col_idx[:, None] * inv_freq], axis=1)
    return jnp.cos(ang), jnp.sin(ang)


def apply_axial_rope(x, cos, sin):
    shp = x.shape
    xs = x.reshape(shp[0], shp[1], shp[2], 2, 2, ROPE_FREQS)
    x1, x2 = xs[..., 0, :], xs[..., 1, :]
    c = cos[None, :, None]
    s = sin[None, :, None]
    return jnp.stack([x1 * c - x2 * s, x2 * c + x1 * s], axis=-2).reshape(shp)


def axial_gqa(q, k, v, q_norm, k_norm, cos, sin):
    bsz, seq, _ = q.shape
    f32 = jnp.float32
    qf = rms_norm(q.astype(f32).reshape(bsz, seq, N_Q_HEADS, HEAD_DIM), q_norm)
    kf = rms_norm(k.astype(f32).reshape(bsz, seq, N_KV_HEADS, HEAD_DIM), k_norm)
    qf = apply_axial_rope(qf, cos, sin) * (HEAD_DIM ** -0.5)
    kf = apply_axial_rope(kf, cos, sin)
    vf = v.astype(f32).reshape(bsz, seq, N_KV_HEADS, HEAD_DIM)
    n_blk = seq // Q_BLOCK
    qb = qf.reshape(bsz, n_blk, Q_BLOCK, N_KV_HEADS, Q_PER_KV, HEAD_DIM).transpose(1, 0, 2, 3, 4, 5)

    def block(q_blk):
        s = jnp.einsum("bqkgd,bskd->bkgqs", q_blk, kf)
        p = jax.nn.softmax(s, axis=-1)
        return jnp.einsum("bkgqs,bskd->bqkgd", p, vf)

    o = lax.map(block, qb)
    o = o.transpose(1, 0, 2, 3, 4, 5).reshape(bsz, seq, ATTN_WIDTH)
    return o.astype(q.dtype)


def rglru_mixer(z, conv_w, conv_b, ra_w, ra_b, ix_w, ix_b, lam):
    bsz, seq, _ = z.shape
    f32 = jnp.float32
    zf = z.astype(f32)
    gate, xr = zf[..., :LRU_WIDTH], zf[..., LRU_WIDTH:]
    xc = lax.conv_general_dilated(
        xr, conv_w.astype(f32)[:, None, :], window_strides=(1,),
        padding=[(CONV_LEFT, CONV_WIDTH - 1 - CONV_LEFT)],
        dimension_numbers=("NWC", "WIO", "NWC"),
        feature_group_count=LRU_WIDTH) + conv_b.astype(f32)
    xb = xc.reshape(bsz, seq, LRU_HEADS, LRU_BLOCK)
    hs = []
    for direction in range(2):
        r = jax.nn.sigmoid(jnp.einsum("bshi,hij->bshj", xb, ra_w[direction].astype(f32)).reshape(bsz, seq, LRU_WIDTH)
                           + ra_b[direction].astype(f32))
        i = jax.nn.sigmoid(jnp.einsum("bshi,hij->bshj", xb, ix_w[direction].astype(f32)).reshape(bsz, seq, LRU_WIDTH)
                           + ix_b[direction].astype(f32))
        log_a = -LRU_C * r * jax.nn.softplus(-lam[direction].astype(f32))
        a = jnp.exp(log_a)
        b = jnp.sqrt(-jnp.expm1(2.0 * log_a)) * (i * xc)
        _, h = lax.associative_scan(_linear_combine, (a, b), reverse=(direction == 1), axis=1)
        hs.append(h)
    y = hs[0] + hs[1]
    return (y * jax.nn.gelu(gate)).astype(z.dtype)


def setup_inputs(seed: int = 0) -> dict:
    key = jax.random.key(seed)
    keys = iter(jax.random.split(key, 48))

    def normal(shape, std):
        return std * jax.random.normal(next(keys), shape, jnp.float32)

    def uniform(shape, lo, hi):
        return jax.random.uniform(next(keys), shape, jnp.float32, lo, hi)

    D = D_MODEL
    x = normal((BATCH, SEQ, D), 1.0)
    c = normal((BATCH, D), 1.0)
    norm_w = 1.0 + normal((DEPTH, 2, D), 0.02)
    ada_w = normal((DEPTH, D, 6 * D), 0.1 * D ** -0.5)
    gate_offset = jnp.tile(jnp.repeat(jnp.array([0.0, 0.0, 1.0], jnp.float32), D), 2)
    ada_b = normal((DEPTH, 6 * D), 0.02) + gate_offset
    mlp_w1 = normal((DEPTH, D, D_FF), D ** -0.5)
    mlp_w2 = normal((DEPTH, D_FF, D), D_FF ** -0.5)
    final_norm_w = 1.0 + normal((D,), 0.02)

    hyb_w_in = normal((N_EVEN, D, HYB_IN), D ** -0.5)
    s5_lam_re = -0.5 + normal((N_EVEN, 2, S5_GROUPS, S5_STATE), 0.01)
    s5_lam_im = jnp.pi * jnp.arange(S5_STATE, dtype=jnp.float32) + normal((N_EVEN, 2, S5_GROUPS, S5_STATE), 0.01)
    s5_log_dt = uniform((N_EVEN, 2, S5_GROUPS), math.log(1e-3), math.log(1e-1))
    s5_b_re = normal((N_EVEN, 2, S5_GROUPS, S5_STATE, S5_GROUP), (2 * S5_GROUP) ** -0.5)
    s5_b_im = normal((N_EVEN, 2, S5_GROUPS, S5_STATE, S5_GROUP), (2 * S5_GROUP) ** -0.5)
    s5_c_re = normal((N_EVEN, 2, S5_GROUPS, S5_GROUP, S5_STATE), S5_STATE ** -0.5)
    s5_c_im = normal((N_EVEN, 2, S5_GROUPS, S5_GROUP, S5_STATE), S5_STATE ** -0.5)
    s5_d = normal((N_EVEN, S5_WIDTH), 1.0)
    s5_glu_w = normal((N_EVEN, S5_WIDTH, S5_WIDTH), S5_WIDTH ** -0.5)
    s5_glu_b = normal((N_EVEN, S5_WIDTH), 0.01)
    attn_q_norm = 1.0 + normal((N_EVEN, HEAD_DIM), 0.02)
    attn_k_norm = 1.0 + normal((N_EVEN, HEAD_DIM), 0.02)
    hyb_w_out = normal((N_EVEN, HYB_OUT, D), HYB_OUT ** -0.5)

    rec_w_in = normal((N_ODD, D, 2 * LRU_WIDTH), D ** -0.5)
    rec_conv_w = normal((N_ODD, CONV_WIDTH, LRU_WIDTH), CONV_WIDTH ** -0.5)
    rec_conv_b = normal((N_ODD, LRU_WIDTH), 0.01)
    rec_ra_w = normal((N_ODD, 2, LRU_HEADS, LRU_BLOCK, LRU_BLOCK), LRU_BLOCK ** -0.5)
    rec_ra_b = normal((N_ODD, 2, LRU_WIDTH), 0.01)
    rec_ix_w = normal((N_ODD, 2, LRU_HEADS, LRU_BLOCK, LRU_BLOCK), LRU_BLOCK ** -0.5)
    rec_ix_b = normal((N_ODD, 2, LRU_WIDTH), 0.01)
    a_c = uniform((N_ODD, 2, LRU_WIDTH), 0.9, 0.999)
    s = a_c ** (1.0 / LRU_C)
    rec_lam = jnp.log(s) - jnp.log1p(-s)
    rec_w_out = normal((N_ODD, LRU_WIDTH, D), LRU_WIDTH ** -0.5)

    return {"x": x, "c": c, "norm_w": norm_w, "ada_w": ada_w, "ada_b": ada_b,
            "mlp_w1": mlp_w1, "mlp_w2": mlp_w2, "final_norm_w": final_norm_w,
            "hyb_w_in": hyb_w_in, "s5_lam_re": s5_lam_re, "s5_lam_im": s5_lam_im,
            "s5_log_dt": s5_log_dt, "s5_b_re": s5_b_re, "s5_b_im": s5_b_im,
            "s5_c_re": s5_c_re, "s5_c_im": s5_c_im, "s5_d": s5_d,
            "s5_glu_w": s5_glu_w, "s5_glu_b": s5_glu_b,
            "attn_q_norm": attn_q_norm, "attn_k_norm": attn_k_norm, "hyb_w_out": hyb_w_out,
            "rec_w_in": rec_w_in, "rec_conv_w": rec_conv_w, "rec_conv_b": rec_conv_b,
            "rec_ra_w": rec_ra_w, "rec_ra_b": rec_ra_b, "rec_ix_w": rec_ix_w,
            "rec_ix_b": rec_ix_b, "rec_lam": rec_lam, "rec_w_out": rec_w_out}


def reference(x, c, norm_w, ada_w, ada_b, mlp_w1, mlp_w2, final_norm_w,
              hyb_w_in, s5_lam_re, s5_lam_im, s5_log_dt, s5_b_re, s5_b_im,
              s5_c_re, s5_c_im, s5_d, s5_glu_w, s5_glu_b,
              attn_q_norm, attn_k_norm, hyb_w_out,
              rec_w_in, rec_conv_w, rec_conv_b, rec_ra_w, rec_ra_b, rec_ix_w,
              rec_ix_b, rec_lam, rec_w_out):
    seq = x.shape[1]
    cos, sin = axial_rope_tables(seq)
    c_act = jax.nn.silu(c)
    h = x
    for layer in range(DEPTH):
        mod = (c_act @ ada_w[layer] + ada_b[layer])[:, None, :]
        sh1, sc1, g1, sh2, sc2, g2 = jnp.split(mod, 6, axis=-1)
        u = rms_norm(h, norm_w[layer, 0]) * (1.0 + sc1) + sh1
        if layer % 2 == 0:
            e = layer // 2
            z = u @ hyb_w_in[e]
            z_s5 = z[..., :S5_WIDTH]
            z_q = z[..., S5_WIDTH:S5_WIDTH + ATTN_WIDTH]
            z_k = z[..., S5_WIDTH + ATTN_WIDTH:S5_WIDTH + ATTN_WIDTH + KV_WIDTH]
            z_v = z[..., S5_WIDTH + ATTN_WIDTH + KV_WIDTH:]
            y_s5 = s5_mixer(z_s5, s5_lam_re[e], s5_lam_im[e], s5_log_dt[e], s5_b_re[e], s5_b_im[e],
                            s5_c_re[e], s5_c_im[e], s5_d[e], s5_glu_w[e], s5_glu_b[e])
            y_att = axial_gqa(z_q, z_k, z_v, attn_q_norm[e], attn_k_norm[e], cos, sin)
            mix = jnp.concatenate([y_s5, y_att], axis=-1) @ hyb_w_out[e]
        else:
            o = layer // 2
            z = u @ rec_w_in[o]
            mix = rglru_mixer(z, rec_conv_w[o], rec_conv_b[o], rec_ra_w[o], rec_ra_b[o],
                              rec_ix_w[o], rec_ix_b[o], rec_lam[o]) @ rec_w_out[o]
        h = h + g1 * mix
        u = rms_norm(h, norm_w[layer, 1]) * (1.0 + sc2) + sh2
        ff = jnp.square(jax.nn.relu(u @ mlp_w1[layer])) @ mlp_w2[layer]
        h = h + g2 * ff
    return rms_norm(h, final_norm_w)
```

```python
import functools

import jax
import jax.numpy as jnp
from jax import lax
from jax.experimental import pallas as pl
from jax.experimental.pallas import tpu as pltpu

F32 = jnp.float32
BF16 = jnp.bfloat16

EPS = 1e-6
GRID_W = 64

S5_GROUP = 16
S5_STATE = 64
S5_MAX_RE = -1e-4
S5_GROUPS_PER_BLOCK = 8
S5_BLOCK_IN = S5_GROUPS_PER_BLOCK * S5_GROUP
S5_BLOCK_STATE = S5_GROUPS_PER_BLOCK * S5_STATE

HEAD_DIM = 64
N_KV_HEADS = 2
ROPE_FREQS = HEAD_DIM // 4
ROPE_BASE = 10000.0

LRU_HEADS = 16
LRU_HEADS_PER_BLOCK = 4
CONV_WIDTH = 4
CONV_LEFT = 2
LRU_C = 8.0

SUBLANES = 8
LANES = 128

TOKEN_TILE = 512
ATTN_Q_TILE = 512
ATTN_KV_TILE = 512
S5_TIME_TILE = 64
LRU_TIME_TILE = 32
FF_TILE = 512
VMEM_LIMIT = 56 * 1024 * 1024


def _cparams(sem):
    return pltpu.CompilerParams(dimension_semantics=sem, vmem_limit_bytes=VMEM_LIMIT)


def _const_spec(shape):
    nd = len(shape)
    return pl.BlockSpec(shape, lambda *_: (0,) * nd, pipeline_mode=pl.Buffered(1))


def _norm_mod(x, w_row, sc_row, sh_row):
    rs = lax.rsqrt(jnp.mean(x * x, axis=-1, keepdims=True) + EPS)
    return x * rs * (w_row * (1.0 + sc_row)) + sh_row


def _sigmoid(x):
    return 1.0 / (1.0 + jnp.exp(-x))


def _gelu(x):
    return jax.nn.gelu(x, approximate=True)


def _one_minus_exp(x, exp_x):
    series = -x * (1.0 + x * (1 / 2 + x * (1 / 6 + x * (1 / 24 + x * (1 / 120 + x * (1 / 720))))))
    return jnp.where(x > -1 / 16, series, 1.0 - exp_x)


def _ada_kernel(c_ref, w_ref, b_ref, o_ref):
    c = c_ref[...]
    ca = (c * _sigmoid(c)).astype(BF16)
    o_ref[0] = jnp.dot(ca, w_ref[0].astype(BF16), preferred_element_type=F32) + b_ref[0]


def _ada_mod(c, ada_w, ada_b):
    depth, d, n = ada_w.shape
    bsz = c.shape[0]
    tn = 1024
    return pl.pallas_call(
        _ada_kernel,
        grid=(depth, n // tn),
        in_specs=[
            pl.BlockSpec((bsz, d), lambda l, j: (0, 0)),
            pl.BlockSpec((1, d, tn), lambda l, j: (l, 0, j)),
            pl.BlockSpec((1, 1, tn), lambda l, j: (l, 0, j)),
        ],
        out_specs=pl.BlockSpec((1, bsz, tn), lambda l, j: (l, 0, j)),
        out_shape=jax.ShapeDtypeStruct((depth, bsz, n), F32),
        compiler_params=_cparams(("arbitrary", "arbitrary")),
        name="ada_mod",
    )(c, ada_w, ada_b.reshape(depth, 1, n))


def _head_norm_rope(z, g_ones, nw_row, cos, sin_signed):
    ss = jnp.dot((z * z).astype(BF16), g_ones, preferred_element_type=F32)
    zn = z * lax.rsqrt(ss * (1.0 / HEAD_DIM) + EPS) * nw_row
    lane = lax.broadcasted_iota(jnp.int32, zn.shape, 1)
    first_half = (lane % (2 * ROPE_FREQS)) < ROPE_FREQS
    partner = jnp.where(first_half,
                        pltpu.roll(zn, LANES - ROPE_FREQS, axis=1),
                        pltpu.roll(zn, ROPE_FREQS, axis=1))
    return zn * cos + partner * sin_signed


def _hyb_in_kernel(h_ref, mod_ref, nw_ref, w_ref, gones_ref, qn_ref, kn_ref, cos_ref, sin_ref,
                   zs5_ref, q_ref, k_ref, v_ref, *, s5_w, q_w, kv_w):
    x = h_ref[0]
    mod = mod_ref[0]
    u = _norm_mod(x, nw_ref[...], mod[1:2], mod[0:1]).astype(BF16)
    z = jnp.dot(u, w_ref[...], preferred_element_type=F32)
    zs5_ref[...] = z[:, :s5_w].astype(BF16)
    cos = cos_ref[...]
    sin = sin_ref[...]
    g_ones = gones_ref[...]
    for j in range(q_w // LANES):
        zq = z[:, s5_w + j * LANES: s5_w + (j + 1) * LANES]
        q_ref[0, :, j * LANES:(j + 1) * LANES] = _head_norm_rope(
            zq, g_ones, qn_ref[...], cos, sin).astype(BF16)
    for j in range(kv_w // LANES):
        zk = z[:, s5_w + q_w + j * LANES: s5_w + q_w + (j + 1) * LANES]
        k_ref[0, :, j * LANES:(j + 1) * LANES] = _head_norm_rope(
            zk, g_ones, kn_ref[...], cos, sin).astype(BF16)
    v_ref[0] = z[:, s5_w + q_w + kv_w:].astype(BF16)


def _hyb_in(h, mod, nw, w_in, g_ones, qn_row, kn_row, cos, sin, s5_w, q_w, kv_w):
    bsz, seq, d = h.shape
    ts = TOKEN_TILE
    n = w_in.shape[1]
    kern = functools.partial(_hyb_in_kernel, s5_w=s5_w, q_w=q_w, kv_w=kv_w)
    return pl.pallas_call(
        kern,
        grid=(bsz, seq // ts),
        in_specs=[
            pl.BlockSpec((1, ts, d), lambda b, s: (b, s, 0)),
            pl.BlockSpec((1, 6, d), lambda b, s: (b, 0, 0)),
            _const_spec((1, d)),
            _const_spec((d, n)),
            _const_spec((LANES, LANES)),
            _const_spec((1, LANES)),
            _const_spec((1, LANES)),
            pl.BlockSpec((ts, LANES), lambda b, s: (s, 0)),
            pl.BlockSpec((ts, LANES), lambda b, s: (s, 0)),
        ],
        out_specs=[
            pl.BlockSpec((ts, s5_w), lambda b, s: (s, b)),
            pl.BlockSpec((1, ts, q_w), lambda b, s: (b, s, 0)),
            pl.BlockSpec((1, ts, kv_w), lambda b, s: (b, s, 0)),
            pl.BlockSpec((1, ts, kv_w), lambda b, s: (b, s, 0)),
        ],
        out_shape=[
            jax.ShapeDtypeStruct((seq, bsz * s5_w), BF16),
            jax.ShapeDtypeStruct((bsz, seq, q_w), BF16),
            jax.ShapeDtypeStruct((bsz, seq, kv_w), BF16),
            jax.ShapeDtypeStruct((bsz, seq, kv_w), BF16),
        ],
        compiler_params=_cparams(("arbitrary", "arbitrary")),
        name="hyb_in",
    )(h, mod, nw, w_in, g_ones, qn_row, kn_row, cos, sin)


def _s5_direction(u_ref, bmat_ref, are_ref, aim_ref, cmat_ref, y_ref, bu_ref, carry_ref, *,
                  tt, n_blk, reverse):
    ub = u_ref[...]
    w = 2 * S5_BLOCK_STATE
    for j in range(n_blk):
        bu_ref[:, j * w:(j + 1) * w] = jnp.dot(
            ub[:, j * S5_BLOCK_IN:(j + 1) * S5_BLOCK_IN], bmat_ref[j], preferred_element_type=F32)
    for j in range(n_blk):
        re_cols = slice(j * w, j * w + S5_BLOCK_STATE)
        im_cols = slice(j * w + S5_BLOCK_STATE, (j + 1) * w)
        a_re = jnp.broadcast_to(are_ref[j:j + 1, :], (SUBLANES, S5_BLOCK_STATE))
        a_im = jnp.broadcast_to(aim_ref[j:j + 1, :], (SUBLANES, S5_BLOCK_STATE))

        def body(k, hc, re_cols=re_cols, im_cols=im_cols, a_re=a_re, a_im=a_im):
            h_re, h_im = hc
            t = (tt - 1 - k) if reverse else k
            rows = pl.ds(pl.multiple_of(t * SUBLANES, SUBLANES), SUBLANES)
            n_re = a_re * h_re - a_im * h_im + bu_ref[rows, re_cols]
            n_im = a_re * h_im + a_im * h_re + bu_ref[rows, im_cols]
            bu_ref[rows, re_cols] = n_re
            bu_ref[rows, im_cols] = n_im
            return n_re, n_im

        h_re, h_im = lax.fori_loop(0, tt, body, (carry_ref[:, re_cols], carry_ref[:, im_cols]), unroll=8)
        carry_ref[:, re_cols] = h_re
        carry_ref[:, im_cols] = h_im
    for j in range(n_blk):
        hb = bu_ref[:, j * w:(j + 1) * w].astype(BF16)
        y_ref[:, j * S5_BLOCK_IN:(j + 1) * S5_BLOCK_IN] = jnp.dot(
            hb, cmat_ref[j], preferred_element_type=F32).astype(BF16)


def _s5_kernel(uf_ref, ub_ref, bmat_ref, are_ref, aim_ref, cmat_ref, yf_ref, yb_ref,
               bu_ref, carry_ref, *, tt, n_blk):
    @pl.when(pl.program_id(0) == 0)
    def _():
        carry_ref[...] = jnp.zeros_like(carry_ref)

    _s5_direction(uf_ref, bmat_ref.at[0], are_ref.at[0], aim_ref.at[0], cmat_ref.at[0], yf_ref,
                  bu_ref, carry_ref.at[0], tt=tt, n_blk=n_blk, reverse=False)
    _s5_direction(ub_ref, bmat_ref.at[1], are_ref.at[1], aim_ref.at[1], cmat_ref.at[1], yb_ref,
                  bu_ref, carry_ref.at[1], tt=tt, n_blk=n_blk, reverse=True)


def _s5_scan(u_tm, bmat, a_re, a_im, cmat):
    rows_total, width = u_tm.shape
    tt = S5_TIME_TILE
    rows = tt * SUBLANES
    nc = rows_total // rows
    n_blk = width // S5_BLOCK_IN
    state_w = n_blk * 2 * S5_BLOCK_STATE
    kern = functools.partial(_s5_kernel, tt=tt, n_blk=n_blk)
    return pl.pallas_call(
        kern,
        grid=(nc,),
        in_specs=[
            pl.BlockSpec((rows, width), lambda i: (i, 0)),
            pl.BlockSpec((rows, width), lambda i: (nc - 1 - i, 0)),
            _const_spec(bmat.shape),
            _const_spec(a_re.shape),
            _const_spec(a_im.shape),
            _const_spec(cmat.shape),
        ],
        out_specs=[
            pl.BlockSpec((rows, width), lambda i: (i, 0)),
            pl.BlockSpec((rows, width), lambda i: (nc - 1 - i, 0)),
        ],
        out_shape=[jax.ShapeDtypeStruct((rows_total, width), BF16)] * 2,
        scratch_shapes=[
            pltpu.VMEM((rows, state_w), F32),
            pltpu.VMEM((2, SUBLANES, state_w), F32),
        ],
        compiler_params=_cparams(("arbitrary",)),
        name="s5_scan",
    )(u_tm, u_tm, bmat, a_re, a_im, cmat)


def _attn_kernel(q_ref, k_ref, v_ref, o_ref, kext_ref, vext_ref, *, seq, tq, tk, q_per_kv):
    pairs_per_kv = q_per_kv // 2

    @pl.when(pl.program_id(1) == 0)
    def _():
        def prep(r, carry):
            rows = pl.ds(pl.multiple_of(r * tk, tk), tk)
            kf = k_ref[0, rows, :].astype(F32)
            vf = v_ref[0, rows, :].astype(F32)
            lane = lax.broadcasted_iota(jnp.int32, kf.shape, 1)
            low = lane < HEAD_DIM
            k_sw = pltpu.roll(kf, HEAD_DIM, axis=1)
            v_sw = pltpu.roll(vf, HEAD_DIM, axis=1)
            kext_ref[0, 0, rows, :] = jnp.where(low, kf, 0.0).astype(BF16)
            kext_ref[0, 1, rows, :] = jnp.where(low, 0.0, k_sw).astype(BF16)
            kext_ref[1, 0, rows, :] = jnp.where(low, k_sw, 0.0).astype(BF16)
            kext_ref[1, 1, rows, :] = jnp.where(low, 0.0, kf).astype(BF16)
            vext_ref[0, 0, rows, :] = jnp.where(low, vf, 1.0).astype(BF16)
            vext_ref[0, 1, rows, :] = jnp.where(low, 1.0, v_sw).astype(BF16)
            vext_ref[1, 0, rows, :] = jnp.where(low, v_sw, 1.0).astype(BF16)
            vext_ref[1, 1, rows, :] = jnp.where(low, 1.0, vf).astype(BF16)
            return carry

        lax.fori_loop(0, seq // tk, prep, 0)

    n_kv = seq // tk
    for kvh in range(N_KV_HEADS):
        for m in range(pairs_per_kv):
            pair = kvh * pairs_per_kv + m
            qp = q_ref[0, :, pair * LANES:(pair + 1) * LANES]

            def body(c, carry, kvh=kvh, qp=qp):
                rows = pl.ds(pl.multiple_of(c * tk, tk), tk)
                out = []
                for e in range(2):
                    m_prev, acc = carry[2 * e], carry[2 * e + 1]
                    s = lax.dot_general(qp, kext_ref[kvh, e, rows, :], (((1,), (1,)), ((), ())),
                                        preferred_element_type=F32)
                    m_new = jnp.maximum(m_prev, jnp.max(s, axis=-1, keepdims=True))
                    p = jnp.exp(s - m_new).astype(BF16)
                    acc = acc * jnp.exp(m_prev - m_new) + jnp.dot(
                        p, vext_ref[kvh, e, rows, :], preferred_element_type=F32)
                    out += [m_new, acc]
                return tuple(out)

            init = (jnp.full((tq, 1), -jnp.inf, F32), jnp.zeros((tq, LANES), F32)) * 2
            _, r0, _, r1 = lax.fori_loop(0, n_kv, body, init)
            lane = lax.broadcasted_iota(jnp.int32, r0.shape, 1)
            o_pair = jnp.where(lane < HEAD_DIM,
                               r0 / pltpu.roll(r0, HEAD_DIM, axis=1),
                               r1 / pltpu.roll(r1, HEAD_DIM, axis=1))
            o_ref[0, :, pair * LANES:(pair + 1) * LANES] = o_pair.astype(BF16)


def _attention(q, k, v):
    bsz, seq, q_w = q.shape
    kv_w = k.shape[-1]
    tq, tk = ATTN_Q_TILE, ATTN_KV_TILE
    q_per_kv = (q_w // HEAD_DIM) // N_KV_HEADS
    kern = functools.partial(_attn_kernel, seq=seq, tq=tq, tk=tk, q_per_kv=q_per_kv)
    return pl.pallas_call(
        kern,
        grid=(bsz, seq // tq),
        in_specs=[
            pl.BlockSpec((1, tq, q_w), lambda b, i: (b, i, 0)),
            pl.BlockSpec((1, seq, kv_w), lambda b, i: (b, 0, 0)),
            pl.BlockSpec((1, seq, kv_w), lambda b, i: (b, 0, 0)),
        ],
        out_specs=pl.BlockSpec((1, tq, q_w), lambda b, i: (b, i, 0)),
        out_shape=jax.ShapeDtypeStruct((bsz, seq, q_w), BF16),
        scratch_shapes=[
            pltpu.VMEM((N_KV_HEADS, 2, seq, LANES), BF16),
            pltpu.VMEM((N_KV_HEADS, 2, seq, LANES), BF16),
        ],
        compiler_params=_cparams(("arbitrary", "arbitrary")),
        name="attention",
    )(q, k, v)


def _residual_mlp_tail(x, mix, mod, nw2_ref, w1_ref, w2_ref, hmid_ref, fw_ref, o_ref, final):
    h1 = x + mod[2:3] * mix
    u = _norm_mod(h1, nw2_ref[...], mod[4:5], mod[3:4]).astype(BF16)
    d_ff = w1_ref.shape[1]
    for f in range(d_ff // FF_TILE):
        cols = slice(f * FF_TILE, (f + 1) * FF_TILE)
        a = jnp.maximum(jnp.dot(u, w1_ref[:, cols], preferred_element_type=F32), 0.0)
        hmid_ref[:, cols] = (a * a).astype(BF16)
    ff = jnp.dot(hmid_ref[...], w2_ref[...], preferred_element_type=F32)
    h2 = h1 + mod[5:6] * ff
    if final:
        rs = lax.rsqrt(jnp.mean(h2 * h2, axis=-1, keepdims=True) + EPS)
        h2 = h2 * rs * fw_ref[...]
    o_ref[0] = h2


def _hyb_out_kernel(zs5_ref, yf_ref, yb_ref, att_ref, h_ref, mod_ref, d_ref, gluw_ref, glub_ref,
                    wout_ref, nw2_ref, w1_ref, w2_ref, fw_ref, o_ref, hmid_ref, *, s5_w, final):
    mod = mod_ref[0]
    y = zs5_ref[...].astype(F32) * d_ref[...] + yf_ref[...].astype(F32) + yb_ref[...].astype(F32)
    g = _gelu(y)
    gate = _sigmoid(jnp.dot(g.astype(BF16), gluw_ref[...], preferred_element_type=F32) + glub_ref[...])
    ys5 = (g * gate).astype(BF16)
    mix = (jnp.dot(ys5, wout_ref[:s5_w, :], preferred_element_type=F32)
           + jnp.dot(att_ref[0], wout_ref[s5_w:, :], preferred_element_type=F32))
    _residual_mlp_tail(h_ref[0], mix, mod, nw2_ref, w1_ref, w2_ref, hmid_ref, fw_ref, o_ref, final)


def _hyb_out(zs5_tm, yf_tm, yb_tm, att, h, mod, d_row, glu_w, glu_b, w_out, nw2, w1, w2, fw, final):
    bsz, seq, d = h.shape
    ts = TOKEN_TILE
    s5_w = glu_w.shape[0]
    d_ff = w1.shape[1]
    tm_spec = pl.BlockSpec((ts, s5_w), lambda b, s: (s, b))
    kern = functools.partial(_hyb_out_kernel, s5_w=s5_w, final=final)
    return pl.pallas_call(
        kern,
        grid=(bsz, seq // ts),
        in_specs=[
            tm_spec, tm_spec, tm_spec,
            pl.BlockSpec((1, ts, att.shape[-1]), lambda b, s: (b, s, 0)),
            pl.BlockSpec((1, ts, d), lambda b, s: (b, s, 0)),
            pl.BlockSpec((1, 6, d), lambda b, s: (b, 0, 0)),
            _const_spec((1, s5_w)),
            _const_spec(glu_w.shape),
            _const_spec((1, s5_w)),
            _const_spec(w_out.shape),
            _const_spec((1, d)),
            _const_spec(w1.shape),
            _const_spec(w2.shape),
            _const_spec((1, d)),
        ],
        out_specs=pl.BlockSpec((1, ts, d), lambda b, s: (b, s, 0)),
        out_shape=jax.ShapeDtypeStruct((bsz, seq, d), F32),
        scratch_shapes=[pltpu.VMEM((ts, d_ff), BF16)],
        compiler_params=_cparams(("arbitrary", "arbitrary")),
        name="hyb_out_mlp",
    )(zs5_tm, yf_tm, yb_tm, att, h, mod, d_row, glu_w, glu_b, w_out, nw2, w1, w2, fw)


def _rec_in_kernel(h_ref, mod_ref, nw_ref, w_ref, gate_ref, xr_ref, *, width):
    mod = mod_ref[0]
    u = _norm_mod(h_ref[0], nw_ref[...], mod[1:2], mod[0:1]).astype(BF16)
    gate_ref[...] = jnp.dot(u, w_ref[:, :width], preferred_element_type=F32).astype(BF16)
    xr_ref[...] = jnp.dot(u, w_ref[:, width:], preferred_element_type=F32)


def _rec_in(h, mod, nw, w_in):
    bsz, seq, d = h.shape
    ts = TOKEN_TILE
    width = w_in.shape[1] // 2
    kern = functools.partial(_rec_in_kernel, width=width)
    tm_spec = pl.BlockSpec((ts, width), lambda b, s: (s, b))
    return pl.pallas_call(
        kern,
        grid=(bsz, seq // ts),
        in_specs=[
            pl.BlockSpec((1, ts, d), lambda b, s: (b, s, 0)),
            pl.BlockSpec((1, 6, d), lambda b, s: (b, 0, 0)),
            _const_spec((1, d)),
            _const_spec(w_in.shape),
        ],
        out_specs=[tm_spec, tm_spec],
        out_shape=[
            jax.ShapeDtypeStruct((seq, bsz * width), BF16),
            jax.ShapeDtypeStruct((seq, bsz * width), F32),
        ],
        compiler_params=_cparams(("arbitrary", "arbitrary")),
        name="rec_in",
    )(h, mod, nw, w_in)


def _lru_direction(main_ref, prev_ref, next_ref, at_start, at_end, convw_ref, convb_ref,
                   wr_ref, wi_ref, rab_ref, ixb_ref, nsp_ref, out_ref,
                   xext_ref, a_ref, b_ref, carry_ref, *, tt, reverse):
    rows = tt * SUBLANES
    halo = CONV_LEFT * SUBLANES
    width = main_ref.shape[1]
    xext_ref[0:halo, :] = jnp.where(at_start, 0.0, prev_ref[...])
    xext_ref[halo:halo + rows, :] = main_ref[...]
    xext_ref[halo + rows:halo + rows + SUBLANES, :] = jnp.where(at_end, 0.0, next_ref[...])
    xc = convb_ref[...]
    for tap in range(CONV_WIDTH):
        xc = xc + convw_ref[tap:tap + 1, :] * xext_ref[tap * SUBLANES:tap * SUBLANES + rows, :]
    xcb = xc.astype(BF16)
    blk = LRU_HEADS_PER_BLOCK * (width // LRU_HEADS)
    for j in range(width // blk):
        cols = slice(j * blk, (j + 1) * blk)
        r = _sigmoid(jnp.dot(xcb[:, cols], wr_ref[j], preferred_element_type=F32) + rab_ref[:, cols])
        gi = _sigmoid(jnp.dot(xcb[:, cols], wi_ref[j], preferred_element_type=F32) + ixb_ref[:, cols])
        log_a = nsp_ref[:, cols] * r
        a = jnp.exp(log_a)
        a_ref[:, cols] = a
        b_ref[:, cols] = jnp.sqrt(_one_minus_exp(2.0 * log_a, a * a)) * (gi * xc[:, cols])

    def body(k, h):
        t = (tt - 1 - k) if reverse else k
        r_t = pl.ds(pl.multiple_of(t * SUBLANES, SUBLANES), SUBLANES)
        h = a_ref[r_t, :] * h + b_ref[r_t, :]
        b_ref[r_t, :] = h
        return h

    carry_ref[...] = lax.fori_loop(0, tt, body, carry_ref[...], unroll=8)
    out_ref[...] = b_ref[...].astype(BF16)


def _lru_kernel(fm_ref, fp_ref, fn_ref, bm_ref, bp_ref, bn_ref, convw_ref, convb_ref,
                wr_ref, wi_ref, rab_ref, ixb_ref, nsp_ref, hf_ref, hb_ref,
                xext_ref, a_ref, b_ref, carry_ref, *, tt):
    i = pl.program_id(0)
    last = pl.num_programs(0) - 1

    @pl.when(i == 0)
    def _():
        carry_ref[...] = jnp.zeros_like(carry_ref)

    _lru_direction(fm_ref, fp_ref, fn_ref, i == 0, i == last, convw_ref, convb_ref,
                   wr_ref.at[0], wi_ref.at[0], rab_ref.at[0], ixb_ref.at[0], nsp_ref.at[0], hf_ref,
                   xext_ref, a_ref, b_ref, carry_ref.at[0], tt=tt, reverse=False)
    _lru_direction(bm_ref, bp_ref, bn_ref, i == last, i == 0, convw_ref, convb_ref,
                   wr_ref.at[1], wi_ref.at[1], rab_ref.at[1], ixb_ref.at[1], nsp_ref.at[1], hb_ref,
                   xext_ref, a_ref, b_ref, carry_ref.at[1], tt=tt, reverse=True)


def _lru_scan(xr_tm, conv_w, conv_b, wr, wi, ra_b, ix_b, nsp):
    rows_total, width = xr_tm.shape
    tt = LRU_TIME_TILE
    rows = tt * SUBLANES
    nc = rows_total // rows
    halo = CONV_LEFT * SUBLANES
    prev_per = rows // halo
    next_per = rows // SUBLANES
    n_next = rows_total // SUBLANES

    def main_f(i):
        return (i, 0)

    def prev_f(i):
        return (jnp.maximum(i * prev_per - 1, 0), 0)

    def next_f(i):
        return (jnp.minimum((i + 1) * next_per, n_next - 1), 0)

    def rev(f):
        return lambda i: f(nc - 1 - i)

    kern = functools.partial(_lru_kernel, tt=tt)
    return pl.pallas_call(
        kern,
        grid=(nc,),
        in_specs=[
            pl.BlockSpec((rows, width), main_f),
            pl.BlockSpec((halo, width), prev_f),
            pl.BlockSpec((SUBLANES, width), next_f),
            pl.BlockSpec((rows, width), rev(main_f)),
            pl.BlockSpec((halo, width), rev(prev_f)),
            pl.BlockSpec((SUBLANES, width), rev(next_f)),
            _const_spec(conv_w.shape),
            _const_spec(conv_b.shape),
            _const_spec(wr.shape),
            _const_spec(wi.shape),
            _const_spec(ra_b.shape),
            _const_spec(ix_b.shape),
            _const_spec(nsp.shape),
        ],
        out_specs=[
            pl.BlockSpec((rows, width), main_f),
            pl.BlockSpec((rows, width), rev(main_f)),
        ],
        out_shape=[jax.ShapeDtypeStruct((rows_total, width), BF16)] * 2,
        scratch_shapes=[
            pltpu.VMEM((rows + halo + SUBLANES, width), F32),
            pltpu.VMEM((rows, width), F32),
            pltpu.VMEM((rows, width), F32),
            pltpu.VMEM((2, SUBLANES, width), F32),
        ],
        compiler_params=_cparams(("arbitrary",)),
        name="lru_scan",
    )(xr_tm, xr_tm, xr_tm, xr_tm, xr_tm, xr_tm, conv_w, conv_b, wr, wi, ra_b, ix_b, nsp)


def _rec_out_kernel(hf_ref, hb_ref, gate_ref, h_ref, mod_ref, wout_ref, nw2_ref, w1_ref, w2_ref,
                    fw_ref, o_ref, hmid_ref, *, final):
    mod = mod_ref[0]
    y = (hf_ref[...].astype(F32) + hb_ref[...].astype(F32)) * _gelu(gate_ref[...].astype(F32))
    mix = jnp.dot(y.astype(BF16), wout_ref[...], preferred_element_type=F32)
    _residual_mlp_tail(h_ref[0], mix, mod, nw2_ref, w1_ref, w2_ref, hmid_ref, fw_ref, o_ref, final)


def _rec_out(hf_tm, hb_tm, gate_tm, h, mod, w_out, nw2, w1, w2, fw, final):
    bsz, seq, d = h.shape
    ts = TOKEN_TILE
    width = w_out.shape[0]
    d_ff = w1.shape[1]
    tm_spec = pl.BlockSpec((ts, width), lambda b, s: (s, b))
    kern = functools.partial(_rec_out_kernel, final=final)
    return pl.pallas_call(
        kern,
        grid=(bsz, seq // ts),
        in_specs=[
            tm_spec, tm_spec, tm_spec,
            pl.BlockSpec((1, ts, d), lambda b, s: (b, s, 0)),
            pl.BlockSpec((1, 6, d), lambda b, s: (b, 0, 0)),
            _const_spec(w_out.shape),
            _const_spec((1, d)),
            _const_spec(w1.shape),
            _const_spec(w2.shape),
            _const_spec((1, d)),
        ],
        out_specs=pl.BlockSpec((1, ts, d), lambda b, s: (b, s, 0)),
        out_shape=jax.ShapeDtypeStruct((bsz, seq, d), F32),
        scratch_shapes=[pltpu.VMEM((ts, d_ff), BF16)],
        compiler_params=_cparams(("arbitrary", "arbitrary")),
        name="rec_out_mlp",
    )(hf_tm, hb_tm, gate_tm, h, mod, w_out, nw2, w1, w2, fw)


def _s5_discretize(lam_re, lam_im, log_dt, b_re, b_im, c_re, c_im):
    lr = jnp.minimum(lam_re, S5_MAX_RE)
    li = lam_im
    dt = jnp.exp(log_dt)[..., None]
    mag = jnp.exp(lr * dt)
    abar_re = mag * jnp.cos(li * dt)
    abar_im = mag * jnp.sin(li * dt)
    den = lr * lr + li * li
    nr = abar_re - 1.0
    f_re = (nr * lr + abar_im * li) / den
    f_im = (abar_im * lr - nr * li) / den
    bb_re = f_re[..., None] * b_re - f_im[..., None] * b_im
    bb_im = f_re[..., None] * b_im + f_im[..., None] * b_re
    n_dir, groups, p, c = bb_re.shape
    gpb = S5_GROUPS_PER_BLOCK
    n_blk = groups // gpb
    eye = jnp.eye(gpb, dtype=F32)

    def pack_b(bb):
        bb = bb.reshape(n_dir, n_blk, gpb, p, c)
        return jnp.einsum("djgpc,gh->djgchp", bb, eye).reshape(n_dir, n_blk, gpb * c, gpb * p)

    def pack_c(cc):
        cc = cc.reshape(n_dir, n_blk, gpb, c, p)
        return jnp.einsum("djgcp,gh->djgphc", cc, eye).reshape(n_dir, n_blk, gpb * p, gpb * c)

    bmat = jnp.concatenate([pack_b(bb_re), pack_b(bb_im)], axis=-1).astype(BF16)
    cmat = jnp.concatenate([pack_c(c_re), -pack_c(c_im)], axis=-2).astype(BF16)
    a_re = abar_re.reshape(n_dir, n_blk, gpb * p)
    a_im = abar_im.reshape(n_dir, n_blk, gpb * p)
    return bmat, a_re, a_im, cmat


def _rope_tables(seq):
    rows = seq // GRID_W
    row_idx = jnp.repeat(jnp.arange(rows, dtype=F32), GRID_W)
    col_idx = jnp.tile(jnp.arange(GRID_W, dtype=F32), rows)
    inv_freq = ROPE_BASE ** (-jnp.arange(ROPE_FREQS, dtype=F32) / ROPE_FREQS)
    ang_r = row_idx[:, None] * inv_freq
    ang_c = col_idx[:, None] * inv_freq
    cos = jnp.concatenate([jnp.cos(ang_r)] * 2 + [jnp.cos(ang_c)] * 2, axis=-1)
    sin = jnp.concatenate([-jnp.sin(ang_r), jnp.sin(ang_r), -jnp.sin(ang_c), jnp.sin(ang_c)], axis=-1)
    reps = LANES // HEAD_DIM
    return jnp.tile(cos, (1, reps)), jnp.tile(sin, (1, reps))


def _block_diag_heads(w):
    n_dir, heads, i, j = w.shape
    hpb = LRU_HEADS_PER_BLOCK
    eye = jnp.eye(hpb, dtype=F32)
    w = w.reshape(n_dir, heads // hpb, hpb, i, j)
    return jnp.einsum("dnhij,hg->dnhigj", w, eye).reshape(n_dir, heads // hpb, hpb * i, hpb * j).astype(BF16)


def kernel(x, c, norm_w, ada_w, ada_b, mlp_w1, mlp_w2, final_norm_w, hyb_w_in, s5_lam_re, s5_lam_im,
           s5_log_dt, s5_b_re, s5_b_im, s5_c_re, s5_c_im, s5_d, s5_glu_w, s5_glu_b, attn_q_norm,
           attn_k_norm, hyb_w_out, rec_w_in, rec_conv_w, rec_conv_b, rec_ra_w, rec_ra_b, rec_ix_w,
           rec_ix_b, rec_lam, rec_w_out):
    bsz, seq, d = x.shape
    depth = norm_w.shape[0]
    assert bsz == SUBLANES, "the time-major recurrent layout needs one sublane tile of batch rows"
    s5_w = s5_d.shape[-1]
    kv_w = N_KV_HEADS * HEAD_DIM
    q_w = hyb_w_in.shape[-1] - s5_w - 2 * kv_w

    mod_all = _ada_mod(c, ada_w, ada_b).reshape(depth, bsz, 6, d)
    cos, sin = _rope_tables(seq)
    reps = LANES // HEAD_DIM
    head_id = jnp.arange(LANES) // HEAD_DIM
    g_ones = (head_id[:, None] == head_id[None, :]).astype(BF16)
    fw = final_norm_w.reshape(1, d)

    h = x
    for layer in range(depth):
        mod = mod_all[layer]
        nw1 = norm_w[layer, 0].reshape(1, d)
        nw2 = norm_w[layer, 1].reshape(1, d)
        w1 = mlp_w1[layer].astype(BF16)
        w2 = mlp_w2[layer].astype(BF16)
        final = layer == depth - 1
        if layer % 2 == 0:
            e = layer // 2
            qn_row = jnp.tile(attn_q_norm[e] * (HEAD_DIM ** -0.5), reps).reshape(1, LANES)
            kn_row = jnp.tile(attn_k_norm[e], reps).reshape(1, LANES)
            zs5_tm, q, k, v = _hyb_in(h, mod, nw1, hyb_w_in[e].astype(BF16), g_ones, qn_row, kn_row,
                                      cos, sin, s5_w, q_w, kv_w)
            bmat, a_re, a_im, cmat = _s5_discretize(s5_lam_re[e], s5_lam_im[e], s5_log_dt[e],
                                                    s5_b_re[e], s5_b_im[e], s5_c_re[e], s5_c_im[e])
            yf, yb = _s5_scan(zs5_tm.reshape(seq * bsz, s5_w), bmat, a_re, a_im, cmat)
            att = _attention(q, k, v)
            h = _hyb_out(zs5_tm, yf.reshape(seq, bsz * s5_w), yb.reshape(seq, bsz * s5_w), att, h, mod,
                         s5_d[e].reshape(1, s5_w), s5_glu_w[e].astype(BF16), s5_glu_b[e].reshape(1, s5_w),
                         hyb_w_out[e].astype(BF16), nw2, w1, w2, fw, final)
        else:
            o = layer // 2
            width = rec_w_out.shape[1]
            gate_tm, xr_tm = _rec_in(h, mod, nw1, rec_w_in[o].astype(BF16))
            nsp = (-LRU_C * jax.nn.softplus(-rec_lam[o])).reshape(2, 1, width)
            hf, hb = _lru_scan(xr_tm.reshape(seq * bsz, width), rec_conv_w[o], rec_conv_b[o].reshape(1, width),
                               _block_diag_heads(rec_ra_w[o]), _block_diag_heads(rec_ix_w[o]),
                               rec_ra_b[o].reshape(2, 1, width), rec_ix_b[o].reshape(2, 1, width), nsp)
            h = _rec_out(hf.reshape(seq, bsz * width), hb.reshape(seq, bsz * width), gate_tm, h, mod,
                         rec_w_out[o].astype(BF16), nw2, w1, w2, fw, final)
    return h
```

```python
import functools
import math

import jax
import jax.numpy as jnp
from jax import lax
from jax.experimental import pallas as pl
from jax.experimental.pallas import tpu as pltpu

F32 = jnp.float32
BF16 = jnp.bfloat16

EPS = 1e-6
GRID_W = 64

S5_GROUP = 16
S5_STATE = 64
S5_MAX_RE = -1e-4
S5_GROUPS_PER_BLOCK = 8
S5_BLOCK_IN = S5_GROUPS_PER_BLOCK * S5_GROUP
S5_BLOCK_STATE = S5_GROUPS_PER_BLOCK * S5_STATE

HEAD_DIM = 64
N_KV_HEADS = 2
ROPE_FREQS = HEAD_DIM // 4
ROPE_BASE = 10000.0

LRU_HEADS = 16
LRU_HEADS_PER_BLOCK = 4
CONV_WIDTH = 4
CONV_LEFT = 2
LRU_C = 8.0

SUBLANES = 8
LANES = 128

TOKEN_TILE = 512
ATTN_Q_TILE = 512
ATTN_KV_TILE = 512
S5_TIME_TILE = 64
LRU_TIME_TILE = 32
FF_TILE = 512
VMEM_LIMIT = 56 * 1024 * 1024

SQRT_FLOOR = 1e-37
LOG2_E = math.log2(math.e)
LN_2 = math.log(2.0)


def _cparams(sem):
    return pltpu.CompilerParams(dimension_semantics=sem, vmem_limit_bytes=VMEM_LIMIT)


def _const_spec(shape):
    nd = len(shape)
    return pl.BlockSpec(shape, lambda *_: (0,) * nd, pipeline_mode=pl.Buffered(1))


def _norm_mod(x, w_row, sc_row, sh_row):
    rs = lax.rsqrt(jnp.mean(x * x, axis=-1, keepdims=True) + EPS)
    return x * rs * (w_row * (1.0 + sc_row)) + sh_row


def _sigmoid(x):
    return 1.0 / (1.0 + jnp.exp(-x))


def _gelu(x):
    return jax.nn.gelu(x, approximate=True)


def _tm_rows(b, tt, first_row=0):
    return pl.ds(first_row + b, tt, stride=SUBLANES)


def _ada_kernel(c_ref, w_ref, b_ref, o_ref):
    c = c_ref[...]
    ca = (c * _sigmoid(c)).astype(BF16)
    o_ref[0] = jnp.dot(ca, w_ref[0].astype(BF16), preferred_element_type=F32) + b_ref[0]


def _ada_mod(c, ada_w, ada_b):
    depth, d, n = ada_w.shape
    bsz = c.shape[0]
    tn = 1024
    return pl.pallas_call(
        _ada_kernel,
        grid=(depth, n // tn),
        in_specs=[
            pl.BlockSpec((bsz, d), lambda l, j: (0, 0)),
            pl.BlockSpec((1, d, tn), lambda l, j: (l, 0, j)),
            pl.BlockSpec((1, 1, tn), lambda l, j: (l, 0, j)),
        ],
        out_specs=pl.BlockSpec((1, bsz, tn), lambda l, j: (l, 0, j)),
        out_shape=jax.ShapeDtypeStruct((depth, bsz, n), F32),
        compiler_params=_cparams(("arbitrary", "arbitrary")),
        name="ada_mod",
    )(c, ada_w, ada_b.reshape(depth, 1, n))


def _head_norm_rope(z, g_ones, nw_row, cos, sin_signed):
    ss = jnp.dot((z * z).astype(BF16), g_ones, preferred_element_type=F32)
    zn = z * lax.rsqrt(ss * (1.0 / HEAD_DIM) + EPS) * nw_row
    lane = lax.broadcasted_iota(jnp.int32, zn.shape, 1)
    first_half = (lane % (2 * ROPE_FREQS)) < ROPE_FREQS
    partner = jnp.where(first_half,
                        pltpu.roll(zn, LANES - ROPE_FREQS, axis=1),
                        pltpu.roll(zn, ROPE_FREQS, axis=1))
    return zn * cos + partner * sin_signed


def _hyb_in_kernel(h_ref, mod_ref, nw_ref, w_ref, gones_ref, qn_ref, kn_ref, cos_ref, sin_ref,
                   zs5_ref, q_ref, k_ref, v_ref, *, s5_w, q_w, kv_w):
    x = h_ref[0]
    mod = mod_ref[0]
    u = _norm_mod(x, nw_ref[...], mod[1:2], mod[0:1]).astype(BF16)
    z = jnp.dot(u, w_ref[...], preferred_element_type=F32)
    zs5_ref[0] = z[:, :s5_w].astype(BF16)
    cos = cos_ref[...]
    sin = sin_ref[...]
    g_ones = gones_ref[...]
    for j in range(q_w // LANES):
        zq = z[:, s5_w + j * LANES: s5_w + (j + 1) * LANES]
        q_ref[0, :, j * LANES:(j + 1) * LANES] = _head_norm_rope(
            zq, g_ones, qn_ref[...], cos, sin).astype(BF16)
    for j in range(kv_w // LANES):
        zk = z[:, s5_w + q_w + j * LANES: s5_w + q_w + (j + 1) * LANES]
        k_ref[0, :, j * LANES:(j + 1) * LANES] = _head_norm_rope(
            zk, g_ones, kn_ref[...], cos, sin).astype(BF16)
    v_ref[0] = z[:, s5_w + q_w + kv_w:].astype(BF16)


def _hyb_in(h, mod, nw, w_in, g_ones, qn_row, kn_row, cos, sin, s5_w, q_w, kv_w):
    bsz, seq, d = h.shape
    ts = TOKEN_TILE
    n = w_in.shape[1]
    kern = functools.partial(_hyb_in_kernel, s5_w=s5_w, q_w=q_w, kv_w=kv_w)

    def tok(width):
        return pl.BlockSpec((1, ts, width), lambda b, s: (b, s, 0))

    return pl.pallas_call(
        kern,
        grid=(bsz, seq // ts),
        in_specs=[
            tok(d),
            pl.BlockSpec((1, 6, d), lambda b, s: (b, 0, 0)),
            _const_spec((1, d)),
            _const_spec((d, n)),
            _const_spec((LANES, LANES)),
            _const_spec((1, LANES)),
            _const_spec((1, LANES)),
            pl.BlockSpec((ts, LANES), lambda b, s: (s, 0)),
            pl.BlockSpec((ts, LANES), lambda b, s: (s, 0)),
        ],
        out_specs=[tok(s5_w), tok(q_w), tok(kv_w), tok(kv_w)],
        out_shape=[
            jax.ShapeDtypeStruct((bsz, seq, s5_w), BF16),
            jax.ShapeDtypeStruct((bsz, seq, q_w), BF16),
            jax.ShapeDtypeStruct((bsz, seq, kv_w), BF16),
            jax.ShapeDtypeStruct((bsz, seq, kv_w), BF16),
        ],
        compiler_params=_cparams(("arbitrary", "arbitrary")),
        name="hyb_in",
    )(h, mod, nw, w_in, g_ones, qn_row, kn_row, cos, sin)


def _s5_direction(u_ref, bmat_ref, are_ref, aim_ref, cmat_ref, y_ref, tm_ref, bu_ref, carry_ref, *,
                  tt, n_blk, reverse):
    bsz = u_ref.shape[0]
    w = 2 * S5_BLOCK_STATE
    for b in range(bsz):
        ub = u_ref[b].astype(F32)
        for j in range(n_blk):
            tm_ref[j, _tm_rows(b, tt), :] = ub[:, j * S5_BLOCK_IN:(j + 1) * S5_BLOCK_IN]
    for j in range(n_blk):
        bu_ref[:, j * w:(j + 1) * w] = jnp.dot(tm_ref[j].astype(BF16), bmat_ref[j], preferred_element_type=F32)
    for j in range(n_blk):
        re_cols = slice(j * w, j * w + S5_BLOCK_STATE)
        im_cols = slice(j * w + S5_BLOCK_STATE, (j + 1) * w)
        a_re = jnp.broadcast_to(are_ref[j:j + 1, :], (SUBLANES, S5_BLOCK_STATE))
        a_im = jnp.broadcast_to(aim_ref[j:j + 1, :], (SUBLANES, S5_BLOCK_STATE))

        def body(k, hc, re_cols=re_cols, im_cols=im_cols, a_re=a_re, a_im=a_im):
            h_re, h_im = hc
            t = (tt - 1 - k) if reverse else k
            rows = pl.ds(pl.multiple_of(t * SUBLANES, SUBLANES), SUBLANES)
            n_re = a_re * h_re - a_im * h_im + bu_ref[rows, re_cols]
            n_im = a_re * h_im + a_im * h_re + bu_ref[rows, im_cols]
            bu_ref[rows, re_cols] = n_re
            bu_ref[rows, im_cols] = n_im
            return n_re, n_im

        h_re, h_im = lax.fori_loop(0, tt, body, (carry_ref[:, re_cols], carry_ref[:, im_cols]), unroll=8)
        carry_ref[:, re_cols] = h_re
        carry_ref[:, im_cols] = h_im
    for j in range(n_blk):
        hb = bu_ref[:, j * w:(j + 1) * w].astype(BF16)
        tm_ref[j] = jnp.dot(hb, cmat_ref[j], preferred_element_type=F32)
    for b in range(bsz):
        y_ref[b] = jnp.concatenate([tm_ref[j, _tm_rows(b, tt), :] for j in range(n_blk)], axis=-1).astype(BF16)


def _s5_kernel(uf_ref, ub_ref, bmat_ref, are_ref, aim_ref, cmat_ref, yf_ref, yb_ref,
               tm_ref, bu_ref, carry_ref, *, tt, n_blk):
    @pl.when(pl.program_id(0) == 0)
    def _():
        carry_ref[...] = jnp.zeros_like(carry_ref)

    _s5_direction(uf_ref, bmat_ref.at[0], are_ref.at[0], aim_ref.at[0], cmat_ref.at[0], yf_ref,
                  tm_ref.at[0], bu_ref.at[0], carry_ref.at[0], tt=tt, n_blk=n_blk, reverse=False)
    _s5_direction(ub_ref, bmat_ref.at[1], are_ref.at[1], aim_ref.at[1], cmat_ref.at[1], yb_ref,
                  tm_ref.at[1], bu_ref.at[1], carry_ref.at[1], tt=tt, n_blk=n_blk, reverse=True)


def _s5_scan(u, bmat, a_re, a_im, cmat):
    bsz, seq, width = u.shape
    tt = S5_TIME_TILE
    rows = tt * bsz
    nc = seq // tt
    n_blk = width // S5_BLOCK_IN
    state_w = n_blk * 2 * S5_BLOCK_STATE
    kern = functools.partial(_s5_kernel, tt=tt, n_blk=n_blk)
    fwd = pl.BlockSpec((bsz, tt, width), lambda i: (0, i, 0))
    bwd = pl.BlockSpec((bsz, tt, width), lambda i: (0, nc - 1 - i, 0))
    return pl.pallas_call(
        kern,
        grid=(nc,),
        in_specs=[fwd, bwd, _const_spec(bmat.shape), _const_spec(a_re.shape), _const_spec(a_im.shape),
                  _const_spec(cmat.shape)],
        out_specs=[fwd, bwd],
        out_shape=[jax.ShapeDtypeStruct((bsz, seq, width), BF16)] * 2,
        scratch_shapes=[
            pltpu.VMEM((2, n_blk, rows, LANES), F32),
            pltpu.VMEM((2, rows, state_w), F32),
            pltpu.VMEM((2, SUBLANES, state_w), F32),
        ],
        compiler_params=_cparams(("arbitrary",)),
        name="s5_scan",
    )(u, u, bmat, a_re, a_im, cmat)


def _attn_kernel(q_ref, k_ref, v_ref, o_ref, kt_ref, vext_ref, s_ref, *, seq, tq, tk, n_heads):
    n_q = seq // tq
    n_kv = seq // tk
    heads_per_kv = n_heads // N_KV_HEADS

    def prep(c, carry):
        rows = pl.ds(pl.multiple_of(c * tk, tk), tk)
        kf = k_ref[0, rows, :].astype(F32)
        vf = v_ref[0, rows, :].astype(F32)
        lane = lax.broadcasted_iota(jnp.int32, kf.shape, 1)
        low = lane < HEAD_DIM
        k_sw = pltpu.roll(kf, HEAD_DIM, axis=1)
        v_sw = pltpu.roll(vf, HEAD_DIM, axis=1)
        kt_ref[0, 0, c] = jnp.where(low, kf, 0.0).T.astype(BF16)
        kt_ref[0, 1, c] = jnp.where(low, 0.0, k_sw).T.astype(BF16)
        kt_ref[1, 0, c] = jnp.where(low, k_sw, 0.0).T.astype(BF16)
        kt_ref[1, 1, c] = jnp.where(low, 0.0, kf).T.astype(BF16)
        vext_ref[0, 0, c] = jnp.where(low, vf, 1.0).astype(BF16)
        vext_ref[0, 1, c] = jnp.where(low, 1.0, v_sw).astype(BF16)
        vext_ref[1, 0, c] = jnp.where(low, v_sw, 1.0).astype(BF16)
        vext_ref[1, 1, c] = jnp.where(low, 1.0, vf).astype(BF16)
        return carry

    lax.fori_loop(0, n_kv, prep, 0)

    def head_ids(h):
        return h // heads_per_kv, h % 2, h // 2

    def scores_chunk(qp, h, c, m_part):
        kvh, e, _ = head_ids(h)
        s = jnp.dot(qp, kt_ref[kvh, e, c], preferred_element_type=F32)
        s_ref[h % 2, c] = s
        for j in range(tk // LANES):
            m_part = jnp.maximum(m_part, s[:, j * LANES:(j + 1) * LANES])
        return m_part

    def pv_chunk(h, c, m_full, acc):
        kvh, e, _ = head_ids(h)
        s = s_ref[h % 2, c]
        p = jnp.concatenate([jnp.exp2(s[:, j * LANES:(j + 1) * LANES] - m_full)
                             for j in range(tk // LANES)], axis=-1).astype(BF16)
        return acc + jnp.dot(p, vext_ref[kvh, e, c], preferred_element_type=F32)

    def row_max(m_part):
        return jnp.broadcast_to(jnp.max(m_part, axis=-1, keepdims=True), (tq, LANES))

    def q_pair(qt, h):
        pair = head_ids(h)[2]
        return q_ref[0, pl.ds(pl.multiple_of(qt * tq, tq), tq), pair * LANES:(pair + 1) * LANES]

    neg = jnp.full((tq, LANES), -jnp.inf, F32)
    zero = jnp.zeros((tq, LANES), F32)

    qp0 = q_pair(0, 0)
    m0 = row_max(lax.fori_loop(0, n_kv, lambda c, m: scores_chunk(qp0, 0, c, m), neg))

    def q_tile(qt, m_head0):
        m_cur = m_head0
        r_even = None
        for h in range(n_heads):
            h_next = (h + 1) % n_heads
            qt_next = qt if h_next else jnp.minimum(qt + 1, n_q - 1)
            qp = q_pair(qt_next, h_next)
            m_part, r = neg, zero
            for c in range(n_kv):
                m_part = scores_chunk(qp, h_next, c, m_part)
                r = pv_chunk(h, c, m_cur, r)
            m_cur = row_max(m_part)
            if h % 2 == 0:
                r_even = r
            else:
                pair = h // 2
                lane = lax.broadcasted_iota(jnp.int32, r.shape, 1)
                o_pair = jnp.where(lane < HEAD_DIM,
                                   r_even / pltpu.roll(r_even, HEAD_DIM, axis=1),
                                   r / pltpu.roll(r, HEAD_DIM, axis=1))
                o_ref[0, pl.ds(pl.multiple_of(qt * tq, tq), tq), pair * LANES:(pair + 1) * LANES] = (
                    o_pair.astype(BF16))
        return m_cur

    lax.fori_loop(0, n_q, q_tile, m0)


def _attention(q, k, v):
    bsz, seq, q_w = q.shape
    kv_w = k.shape[-1]
    tq, tk = ATTN_Q_TILE, ATTN_KV_TILE
    n_heads = q_w // HEAD_DIM
    kern = functools.partial(_attn_kernel, seq=seq, tq=tq, tk=tk, n_heads=n_heads)

    def row(width):
        return pl.BlockSpec((1, seq, width), lambda b: (b, 0, 0))

    return pl.pallas_call(
        kern,
        grid=(bsz,),
        in_specs=[row(q_w), row(kv_w), row(kv_w)],
        out_specs=row(q_w),
        out_shape=jax.ShapeDtypeStruct((bsz, seq, q_w), BF16),
        scratch_shapes=[
            pltpu.VMEM((N_KV_HEADS, 2, seq // tk, LANES, tk), BF16),
            pltpu.VMEM((N_KV_HEADS, 2, seq // tk, tk, LANES), BF16),
            pltpu.VMEM((2, seq // tk, tq, tk), F32),
        ],
        compiler_params=_cparams(("arbitrary",)),
        name="attention",
    )(q, k, v)


def _residual_mlp_tail(x, mix, mod, nw2_ref, w1_ref, w2_ref, hmid_ref, fw_ref, o_ref, final):
    h1 = x + mod[2:3] * mix
    u = _norm_mod(h1, nw2_ref[...], mod[4:5], mod[3:4]).astype(BF16)
    d_ff = w1_ref.shape[1]
    for f in range(d_ff // FF_TILE):
        cols = slice(f * FF_TILE, (f + 1) * FF_TILE)
        a = jnp.maximum(jnp.dot(u, w1_ref[:, cols], preferred_element_type=F32), 0.0)
        hmid_ref[:, cols] = (a * a).astype(BF16)
    ff = jnp.dot(hmid_ref[...], w2_ref[...], preferred_element_type=F32)
    h2 = h1 + mod[5:6] * ff
    if final:
        rs = lax.rsqrt(jnp.mean(h2 * h2, axis=-1, keepdims=True) + EPS)
        h2 = h2 * rs * fw_ref[...]
    o_ref[0] = h2


def _hyb_out_kernel(zs5_ref, yf_ref, yb_ref, att_ref, h_ref, mod_ref, d_ref, gluw_ref, glub_ref,
                    wout_ref, nw2_ref, w1_ref, w2_ref, fw_ref, o_ref, hmid_ref, *, s5_w, final):
    mod = mod_ref[0]
    y = zs5_ref[0].astype(F32) * d_ref[...] + yf_ref[0].astype(F32) + yb_ref[0].astype(F32)
    g = _gelu(y)
    gate = _sigmoid(jnp.dot(g.astype(BF16), gluw_ref[...], preferred_element_type=F32) + glub_ref[...])
    ys5 = (g * gate).astype(BF16)
    mix = (jnp.dot(ys5, wout_ref[:s5_w, :], preferred_element_type=F32)
           + jnp.dot(att_ref[0], wout_ref[s5_w:, :], preferred_element_type=F32))
    _residual_mlp_tail(h_ref[0], mix, mod, nw2_ref, w1_ref, w2_ref, hmid_ref, fw_ref, o_ref, final)


def _hyb_out(zs5, yf, yb, att, h, mod, d_row, glu_w, glu_b, w_out, nw2, w1, w2, fw, final):
    bsz, seq, d = h.shape
    ts = TOKEN_TILE
    s5_w = glu_w.shape[0]
    d_ff = w1.shape[1]

    def tok(width):
        return pl.BlockSpec((1, ts, width), lambda b, s: (b, s, 0))

    kern = functools.partial(_hyb_out_kernel, s5_w=s5_w, final=final)
    return pl.pallas_call(
        kern,
        grid=(bsz, seq // ts),
        in_specs=[
            tok(s5_w), tok(s5_w), tok(s5_w), tok(att.shape[-1]), tok(d),
            pl.BlockSpec((1, 6, d), lambda b, s: (b, 0, 0)),
            _const_spec((1, s5_w)),
            _const_spec(glu_w.shape),
            _const_spec((1, s5_w)),
            _const_spec(w_out.shape),
            _const_spec((1, d)),
            _const_spec(w1.shape),
            _const_spec(w2.shape),
            _const_spec((1, d)),
        ],
        out_specs=tok(d),
        out_shape=jax.ShapeDtypeStruct((bsz, seq, d), F32),
        scratch_shapes=[pltpu.VMEM((ts, d_ff), BF16)],
        compiler_params=_cparams(("arbitrary", "arbitrary")),
        name="hyb_out_mlp",
    )(zs5, yf, yb, att, h, mod, d_row, glu_w, glu_b, w_out, nw2, w1, w2, fw)


def _rec_in_kernel(h_ref, mod_ref, nw_ref, w_ref, gate_ref, xr_ref, *, width):
    mod = mod_ref[0]
    u = _norm_mod(h_ref[0], nw_ref[...], mod[1:2], mod[0:1]).astype(BF16)
    gate_ref[0] = jnp.dot(u, w_ref[:, :width], preferred_element_type=F32).astype(BF16)
    xr_ref[0] = jnp.dot(u, w_ref[:, width:], preferred_element_type=F32)


def _rec_in(h, mod, nw, w_in):
    bsz, seq, d = h.shape
    ts = TOKEN_TILE
    width = w_in.shape[1] // 2
    kern = functools.partial(_rec_in_kernel, width=width)

    def tok(w):
        return pl.BlockSpec((1, ts, w), lambda b, s: (b, s, 0))

    return pl.pallas_call(
        kern,
        grid=(bsz, seq // ts),
        in_specs=[
            tok(d),
            pl.BlockSpec((1, 6, d), lambda b, s: (b, 0, 0)),
            _const_spec((1, d)),
            _const_spec(w_in.shape),
        ],
        out_specs=[tok(width), tok(width)],
        out_shape=[
            jax.ShapeDtypeStruct((bsz, seq, width), BF16),
            jax.ShapeDtypeStruct((bsz, seq, width), F32),
        ],
        compiler_params=_cparams(("arbitrary", "arbitrary")),
        name="rec_in",
    )(h, mod, nw, w_in)


def _quarter_one_minus_exp(x, log2_a, a):
    series = (-0.5 * LN_2) * log2_a * (1.0 + x * (1 / 2 + x * (1 / 6 + x * (1 / 24))))
    return jnp.where(x > -1 / 32, series, 0.25 - 0.25 * (a * a))


def _lru_direction(main_ref, prev_ref, next_ref, at_start, at_end, convw_ref, convb_ref,
                   wr_ref, wi_ref, rab_ref, ixb_ref, nsp_ref, out_ref,
                   xext_ref, a_ref, hs_ref, carry_ref, *, tt, reverse):
    bsz = main_ref.shape[0]
    rows = tt * SUBLANES
    halo_t = prev_ref.shape[1]
    halo = halo_t * SUBLANES
    width = main_ref.shape[2]
    n_lane_blk = width // LANES
    for b in range(bsz):
        xp = jnp.where(at_start, 0.0, prev_ref[b])
        xm = main_ref[b]
        xn = jnp.where(at_end, 0.0, next_ref[b])
        for l in range(n_lane_blk):
            lanes = slice(l * LANES, (l + 1) * LANES)
            xext_ref[l, _tm_rows(b, halo_t), :] = xp[:, lanes]
            xext_ref[l, _tm_rows(b, tt, halo), :] = xm[:, lanes]
            xext_ref[l, _tm_rows(b, halo_t, halo + rows), :] = xn[:, lanes]
    blk = LRU_HEADS_PER_BLOCK * (width // LRU_HEADS)
    lane_per_blk = blk // LANES
    for j in range(width // blk):
        cols = slice(j * blk, (j + 1) * blk)
        parts = []
        for l in range(j * lane_per_blk, (j + 1) * lane_per_blk):
            lanes = slice(l * LANES, (l + 1) * LANES)
            acc = convb_ref[:, lanes]
            for tap in range(CONV_WIDTH):
                first = halo + (tap - CONV_LEFT) * SUBLANES
                acc = acc + convw_ref[tap:tap + 1, lanes] * xext_ref[l, first:first + rows, :]
            parts.append(acc)
        xc = jnp.concatenate(parts, axis=-1)
        xcb = xc.astype(BF16)
        t_r = jnp.tanh(jnp.dot(xcb, wr_ref[j], preferred_element_type=F32) + rab_ref[:, cols])
        t_i = jnp.tanh(jnp.dot(xcb, wi_ref[j], preferred_element_type=F32) + ixb_ref[:, cols])
        log2_a = nsp_ref[:, cols] * (t_r + 1.0)
        a = jnp.exp2(log2_a)
        x = (2.0 * LN_2) * log2_a
        om = _quarter_one_minus_exp(x, log2_a, a)
        bt = (om * lax.rsqrt(jnp.maximum(om, SQRT_FLOOR))) * ((t_i + 1.0) * xc)
        for k in range(lane_per_blk):
            l = j * lane_per_blk + k
            a_ref[l] = a[:, k * LANES:(k + 1) * LANES]
            hs_ref[l] = bt[:, k * LANES:(k + 1) * LANES]

    def body(k, h):
        t = (tt - 1 - k) if reverse else k
        r_t = pl.ds(pl.multiple_of(t * SUBLANES, SUBLANES), SUBLANES)
        new = []
        for l in range(n_lane_blk):
            h_l = a_ref[l, r_t, :] * h[l] + hs_ref[l, r_t, :]
            hs_ref[l, r_t, :] = h_l
            new.append(h_l)
        return tuple(new)

    h0 = tuple(carry_ref[:, l * LANES:(l + 1) * LANES] for l in range(n_lane_blk))
    h_end = lax.fori_loop(0, tt, body, h0, unroll=8)
    for l in range(n_lane_blk):
        carry_ref[:, l * LANES:(l + 1) * LANES] = h_end[l]
    for b in range(bsz):
        out_ref[b] = jnp.concatenate(
            [hs_ref[l, _tm_rows(b, tt), :] for l in range(n_lane_blk)], axis=-1).astype(BF16)


def _lru_kernel(fm_ref, fp_ref, fn_ref, bm_ref, bp_ref, bn_ref, convw_ref, convb_ref,
                wr_ref, wi_ref, rab_ref, ixb_ref, nsp_ref, hf_ref, hb_ref,
                xext_ref, a_ref, hs_ref, carry_ref, *, tt):
    i = pl.program_id(0)
    last = pl.num_programs(0) - 1

    @pl.when(i == 0)
    def _():
        carry_ref[...] = jnp.zeros_like(carry_ref)

    _lru_direction(fm_ref, fp_ref, fn_ref, i == 0, i == last, convw_ref, convb_ref,
                   wr_ref.at[0], wi_ref.at[0], rab_ref.at[0], ixb_ref.at[0], nsp_ref.at[0], hf_ref,
                   xext_ref, a_ref, hs_ref, carry_ref.at[0], tt=tt, reverse=False)
    _lru_direction(bm_ref, bp_ref, bn_ref, i == last, i == 0, convw_ref, convb_ref,
                   wr_ref.at[1], wi_ref.at[1], rab_ref.at[1], ixb_ref.at[1], nsp_ref.at[1], hb_ref,
                   xext_ref, a_ref, hs_ref, carry_ref.at[1], tt=tt, reverse=True)


def _lru_scan(xr, conv_w, conv_b, wr, wi, ra_b, ix_b, nsp):
    bsz, seq, width = xr.shape
    tt = LRU_TIME_TILE
    rows = tt * bsz
    nc = seq // tt
    halo_t = SUBLANES
    per = tt // halo_t
    n_halo = seq // halo_t

    def main_f(i):
        return (0, i, 0)

    def prev_f(i):
        return (0, jnp.maximum(i * per - 1, 0), 0)

    def next_f(i):
        return (0, jnp.minimum((i + 1) * per, n_halo - 1), 0)

    def rev(f):
        return lambda i: f(nc - 1 - i)

    main_blk = (bsz, tt, width)
    halo_blk = (bsz, halo_t, width)
    kern = functools.partial(_lru_kernel, tt=tt)
    return pl.pallas_call(
        kern,
        grid=(nc,),
        in_specs=[
            pl.BlockSpec(main_blk, main_f),
            pl.BlockSpec(halo_blk, prev_f),
            pl.BlockSpec(halo_blk, next_f),
            pl.BlockSpec(main_blk, rev(main_f)),
            pl.BlockSpec(halo_blk, rev(prev_f)),
            pl.BlockSpec(halo_blk, rev(next_f)),
            _const_spec(conv_w.shape),
            _const_spec(conv_b.shape),
            _const_spec(wr.shape),
            _const_spec(wi.shape),
            _const_spec(ra_b.shape),
            _const_spec(ix_b.shape),
            _const_spec(nsp.shape),
        ],
        out_specs=[
            pl.BlockSpec(main_blk, main_f),
            pl.BlockSpec(main_blk, rev(main_f)),
        ],
        out_shape=[jax.ShapeDtypeStruct((bsz, seq, width), BF16)] * 2,
        scratch_shapes=[
            pltpu.VMEM((width // LANES, rows + 2 * halo_t * SUBLANES, LANES), F32),
            pltpu.VMEM((width // LANES, rows, LANES), F32),
            pltpu.VMEM((width // LANES, rows, LANES), F32),
            pltpu.VMEM((2, SUBLANES, width), F32),
        ],
        compiler_params=_cparams(("arbitrary",)),
        name="lru_scan",
    )(xr, xr, xr, xr, xr, xr, conv_w, conv_b, wr, wi, ra_b, ix_b, nsp)


def _rec_out_kernel(hf_ref, hb_ref, gate_ref, h_ref, mod_ref, wout_ref, nw2_ref, w1_ref, w2_ref,
                    fw_ref, o_ref, hmid_ref, *, final):
    mod = mod_ref[0]
    y = (hf_ref[0].astype(F32) + hb_ref[0].astype(F32)) * _gelu(gate_ref[0].astype(F32))
    mix = jnp.dot(y.astype(BF16), wout_ref[...], preferred_element_type=F32)
    _residual_mlp_tail(h_ref[0], mix, mod, nw2_ref, w1_ref, w2_ref, hmid_ref, fw_ref, o_ref, final)


def _rec_out(hf, hb, gate, h, mod, w_out, nw2, w1, w2, fw, final):
    bsz, seq, d = h.shape
    ts = TOKEN_TILE
    width = w_out.shape[0]
    d_ff = w1.shape[1]

    def tok(w):
        return pl.BlockSpec((1, ts, w), lambda b, s: (b, s, 0))

    kern = functools.partial(_rec_out_kernel, final=final)
    return pl.pallas_call(
        kern,
        grid=(bsz, seq // ts),
        in_specs=[
            tok(width), tok(width), tok(width), tok(d),
            pl.BlockSpec((1, 6, d), lambda b, s: (b, 0, 0)),
            _const_spec(w_out.shape),
            _const_spec((1, d)),
            _const_spec(w1.shape),
            _const_spec(w2.shape),
            _const_spec((1, d)),
        ],
        out_specs=tok(d),
        out_shape=jax.ShapeDtypeStruct((bsz, seq, d), F32),
        scratch_shapes=[pltpu.VMEM((ts, d_ff), BF16)],
        compiler_params=_cparams(("arbitrary", "arbitrary")),
        name="rec_out_mlp",
    )(hf, hb, gate, h, mod, w_out, nw2, w1, w2, fw)


def _s5_discretize(lam_re, lam_im, log_dt, b_re, b_im, c_re, c_im):
    lr = jnp.minimum(lam_re, S5_MAX_RE)
    li = lam_im
    dt = jnp.exp(log_dt)[..., None]
    mag = jnp.exp(lr * dt)
    abar_re = mag * jnp.cos(li * dt)
    abar_im = mag * jnp.sin(li * dt)
    den = lr * lr + li * li
    nr = abar_re - 1.0
    f_re = (nr * lr + abar_im * li) / den
    f_im = (abar_im * lr - nr * li) / den
    bb_re = f_re[..., None] * b_re - f_im[..., None] * b_im
    bb_im = f_re[..., None] * b_im + f_im[..., None] * b_re
    n_dir, groups, p, c = bb_re.shape
    gpb = S5_GROUPS_PER_BLOCK
    n_blk = groups // gpb
    eye = jnp.eye(gpb, dtype=F32)

    def pack_b(bb):
        bb = bb.reshape(n_dir, n_blk, gpb, p, c)
        return jnp.einsum("djgpc,gh->djgchp", bb, eye).reshape(n_dir, n_blk, gpb * c, gpb * p)

    def pack_c(cc):
        cc = cc.reshape(n_dir, n_blk, gpb, c, p)
        return jnp.einsum("djgcp,gh->djgphc", cc, eye).reshape(n_dir, n_blk, gpb * p, gpb * c)

    bmat = jnp.concatenate([pack_b(bb_re), pack_b(bb_im)], axis=-1).astype(BF16)
    cmat = jnp.concatenate([pack_c(c_re), -pack_c(c_im)], axis=-2).astype(BF16)
    a_re = abar_re.reshape(n_dir, n_blk, gpb * p)
    a_im = abar_im.reshape(n_dir, n_blk, gpb * p)
    return bmat, a_re, a_im, cmat


def _rope_tables(seq):
    rows = seq // GRID_W
    row_idx = jnp.repeat(jnp.arange(rows, dtype=F32), GRID_W)
    col_idx = jnp.tile(jnp.arange(GRID_W, dtype=F32), rows)
    inv_freq = ROPE_BASE ** (-jnp.arange(ROPE_FREQS, dtype=F32) / ROPE_FREQS)
    ang_r = row_idx[:, None] * inv_freq
    ang_c = col_idx[:, None] * inv_freq
    cos = jnp.concatenate([jnp.cos(ang_r)] * 2 + [jnp.cos(ang_c)] * 2, axis=-1)
    sin = jnp.concatenate([-jnp.sin(ang_r), jnp.sin(ang_r), -jnp.sin(ang_c), jnp.sin(ang_c)], axis=-1)
    reps = LANES // HEAD_DIM
    return jnp.tile(cos, (1, reps)), jnp.tile(sin, (1, reps))


def _block_diag_heads(w):
    n_dir, heads, i, j = w.shape
    hpb = LRU_HEADS_PER_BLOCK
    eye = jnp.eye(hpb, dtype=F32)
    w = w.reshape(n_dir, heads // hpb, hpb, i, j)
    return jnp.einsum("dnhij,hg->dnhigj", w, eye).reshape(n_dir, heads // hpb, hpb * i, hpb * j).astype(BF16)


def kernel(x, c, norm_w, ada_w, ada_b, mlp_w1, mlp_w2, final_norm_w, hyb_w_in, s5_lam_re, s5_lam_im,
           s5_log_dt, s5_b_re, s5_b_im, s5_c_re, s5_c_im, s5_d, s5_glu_w, s5_glu_b, attn_q_norm,
           attn_k_norm, hyb_w_out, rec_w_in, rec_conv_w, rec_conv_b, rec_ra_w, rec_ra_b, rec_ix_w,
           rec_ix_b, rec_lam, rec_w_out):
    bsz, seq, d = x.shape
    depth = norm_w.shape[0]
    assert bsz == SUBLANES, "the time-major recurrent layout needs one sublane tile of batch rows"
    s5_w = s5_d.shape[-1]
    kv_w = N_KV_HEADS * HEAD_DIM
    q_w = hyb_w_in.shape[-1] - s5_w - 2 * kv_w

    mod_all = _ada_mod(c, ada_w, ada_b).reshape(depth, bsz, 6, d)
    cos, sin = _rope_tables(seq)
    reps = LANES // HEAD_DIM
    head_id = jnp.arange(LANES) // HEAD_DIM
    g_ones = (head_id[:, None] == head_id[None, :]).astype(BF16)
    fw = final_norm_w.reshape(1, d)

    h = x
    for layer in range(depth):
        mod = mod_all[layer]
        nw1 = norm_w[layer, 0].reshape(1, d)
        nw2 = norm_w[layer, 1].reshape(1, d)
        w1 = mlp_w1[layer].astype(BF16)
        w2 = mlp_w2[layer].astype(BF16)
        final = layer == depth - 1
        if layer % 2 == 0:
            e = layer // 2
            qn_row = jnp.tile(attn_q_norm[e] * (HEAD_DIM ** -0.5 * LOG2_E), reps).reshape(1, LANES)
            kn_row = jnp.tile(attn_k_norm[e], reps).reshape(1, LANES)
            zs5, q, k, v = _hyb_in(h, mod, nw1, hyb_w_in[e].astype(BF16), g_ones, qn_row, kn_row,
                                   cos, sin, s5_w, q_w, kv_w)
            bmat, a_re, a_im, cmat = _s5_discretize(s5_lam_re[e], s5_lam_im[e], s5_log_dt[e],
                                                    s5_b_re[e], s5_b_im[e], s5_c_re[e], s5_c_im[e])
            yf, yb = _s5_scan(zs5, bmat, a_re, a_im, cmat)
            att = _attention(q, k, v)
            h = _hyb_out(zs5, yf, yb, att, h, mod, s5_d[e].reshape(1, s5_w), s5_glu_w[e].astype(BF16),
                         s5_glu_b[e].reshape(1, s5_w), hyb_w_out[e].astype(BF16), nw2, w1, w2, fw, final)
        else:
            o = layer // 2
            width = rec_w_out.shape[1]
            gate, xr = _rec_in(h, mod, nw1, rec_w_in[o].astype(BF16))
            nsp = ((-0.5 * LRU_C * LOG2_E) * jax.nn.softplus(-rec_lam[o])).reshape(2, 1, width)
            hf, hb = _lru_scan(xr, rec_conv_w[o], rec_conv_b[o].reshape(1, width),
                               _block_diag_heads(0.5 * rec_ra_w[o]), _block_diag_heads(0.5 * rec_ix_w[o]),
                               (0.5 * rec_ra_b[o]).reshape(2, 1, width), (0.5 * rec_ix_b[o]).reshape(2, 1, width),
                               nsp)
            h = _rec_out(hf, hb, gate, h, mod, rec_w_out[o].astype(BF16), nw2, w1, w2, fw, final)
    return h
```

```python
import functools
import math

import jax
import jax.numpy as jnp
from jax import lax
from jax.experimental import pallas as pl
from jax.experimental.pallas import tpu as pltpu

F32 = jnp.float32
BF16 = jnp.bfloat16

EPS = 1e-6
GRID_W = 64

S5_GROUP = 16
S5_STATE = 64
S5_MAX_RE = -1e-4
S5_GROUPS_PER_BLOCK = 8
S5_BLOCK_IN = S5_GROUPS_PER_BLOCK * S5_GROUP
S5_BLOCK_STATE = S5_GROUPS_PER_BLOCK * S5_STATE

HEAD_DIM = 64
N_KV_HEADS = 2
ROPE_FREQS = HEAD_DIM // 4
ROPE_BASE = 10000.0

LRU_HEADS = 16
LRU_HEADS_PER_BLOCK = 4
CONV_WIDTH = 4
CONV_LEFT = 2
LRU_C = 8.0

SUBLANES = 8
LANES = 128

TOKEN_TILE = 512
ATTN_Q_TILE = 512
ATTN_KV_TILE = 512
S5_TIME_TILE = 64
LRU_TIME_TILE = 32
LRU_GATE_ROWS = 64
FF_TILE = 512
VMEM_LIMIT = 56 * 1024 * 1024

SQRT_FLOOR = 1e-37
LOG2_E = math.log2(math.e)
LN_2 = math.log(2.0)


def _cparams(sem):
    return pltpu.CompilerParams(dimension_semantics=sem, vmem_limit_bytes=VMEM_LIMIT)


def _const_spec(shape):
    nd = len(shape)
    return pl.BlockSpec(shape, lambda *_: (0,) * nd, pipeline_mode=pl.Buffered(1))


def _norm_mod(x, w_row, sc_row, sh_row):
    rs = lax.rsqrt(jnp.mean(x * x, axis=-1, keepdims=True) + EPS)
    return x * rs * (w_row * (1.0 + sc_row)) + sh_row


def _sigmoid(x):
    return 1.0 / (1.0 + jnp.exp(-x))


def _gelu(x):
    return jax.nn.gelu(x, approximate=True)


def _tm_rows(b, tt, first_row=0):
    return pl.ds(first_row + b, tt, stride=SUBLANES)


def _ada_kernel(c_ref, w_ref, b_ref, o_ref):
    c = c_ref[...]
    ca = (c * _sigmoid(c)).astype(BF16)
    o_ref[0] = jnp.dot(ca, w_ref[0].astype(BF16), preferred_element_type=F32) + b_ref[0]


def _ada_mod(c, ada_w, ada_b):
    depth, d, n = ada_w.shape
    bsz = c.shape[0]
    tn = 1024
    return pl.pallas_call(
        _ada_kernel,
        grid=(depth, n // tn),
        in_specs=[
            pl.BlockSpec((bsz, d), lambda l, j: (0, 0)),
            pl.BlockSpec((1, d, tn), lambda l, j: (l, 0, j)),
            pl.BlockSpec((1, 1, tn), lambda l, j: (l, 0, j)),
        ],
        out_specs=pl.BlockSpec((1, bsz, tn), lambda l, j: (l, 0, j)),
        out_shape=jax.ShapeDtypeStruct((depth, bsz, n), F32),
        compiler_params=_cparams(("arbitrary", "arbitrary")),
        name="ada_mod",
    )(c, ada_w, ada_b.reshape(depth, 1, n))


def _head_norm_rope(z, g_ones, nw_row, cos, sin_signed):
    ss = jnp.dot((z * z).astype(BF16), g_ones, preferred_element_type=F32)
    zn = z * lax.rsqrt(ss * (1.0 / HEAD_DIM) + EPS) * nw_row
    lane = lax.broadcasted_iota(jnp.int32, zn.shape, 1)
    first_half = (lane % (2 * ROPE_FREQS)) < ROPE_FREQS
    partner = jnp.where(first_half,
                        pltpu.roll(zn, LANES - ROPE_FREQS, axis=1),
                        pltpu.roll(zn, ROPE_FREQS, axis=1))
    return zn * cos + partner * sin_signed


def _hyb_in_kernel(h_ref, mod_ref, nw_ref, w_ref, gones_ref, qn_ref, kn_ref, cos_ref, sin_ref,
                   zs5_ref, q_ref, k_ref, v_ref, *, s5_w, q_w, kv_w):
    x = h_ref[0]
    mod = mod_ref[0]
    u = _norm_mod(x, nw_ref[...], mod[1:2], mod[0:1]).astype(BF16)
    z = jnp.dot(u, w_ref[...], preferred_element_type=F32)
    zs5_ref[0] = z[:, :s5_w].astype(BF16)
    cos = cos_ref[...]
    sin = sin_ref[...]
    g_ones = gones_ref[...]
    for j in range(q_w // LANES):
        zq = z[:, s5_w + j * LANES: s5_w + (j + 1) * LANES]
        q_ref[0, :, j * LANES:(j + 1) * LANES] = _head_norm_rope(
            zq, g_ones, qn_ref[...], cos, sin).astype(BF16)
    for j in range(kv_w // LANES):
        zk = z[:, s5_w + q_w + j * LANES: s5_w + q_w + (j + 1) * LANES]
        k_ref[0, :, j * LANES:(j + 1) * LANES] = _head_norm_rope(
            zk, g_ones, kn_ref[...], cos, sin).astype(BF16)
    v_ref[0] = z[:, s5_w + q_w + kv_w:].astype(BF16)


def _hyb_in(h, mod, nw, w_in, g_ones, qn_row, kn_row, cos, sin, s5_w, q_w, kv_w):
    bsz, seq, d = h.shape
    ts = TOKEN_TILE
    n = w_in.shape[1]
    kern = functools.partial(_hyb_in_kernel, s5_w=s5_w, q_w=q_w, kv_w=kv_w)

    def tok(width):
        return pl.BlockSpec((1, ts, width), lambda b, s: (b, s, 0))

    return pl.pallas_call(
        kern,
        grid=(bsz, seq // ts),
        in_specs=[
            tok(d),
            pl.BlockSpec((1, 6, d), lambda b, s: (b, 0, 0)),
            _const_spec((1, d)),
            _const_spec((d, n)),
            _const_spec((LANES, LANES)),
            _const_spec((1, LANES)),
            _const_spec((1, LANES)),
            pl.BlockSpec((ts, LANES), lambda b, s: (s, 0)),
            pl.BlockSpec((ts, LANES), lambda b, s: (s, 0)),
        ],
        out_specs=[tok(s5_w), tok(q_w), tok(kv_w), tok(kv_w)],
        out_shape=[
            jax.ShapeDtypeStruct((bsz, seq, s5_w), BF16),
            jax.ShapeDtypeStruct((bsz, seq, q_w), BF16),
            jax.ShapeDtypeStruct((bsz, seq, kv_w), BF16),
            jax.ShapeDtypeStruct((bsz, seq, kv_w), BF16),
        ],
        compiler_params=_cparams(("arbitrary", "arbitrary")),
        name="hyb_in",
    )(h, mod, nw, w_in, g_ones, qn_row, kn_row, cos, sin)


def _s5_kernel(u0_ref, u1_ref, bmat_ref, are_ref, aim_ref, cmat_ref, y0_ref, y1_ref,
               tm_ref, bu_ref, carry_ref, *, tt, n_blk):
    @pl.when(pl.program_id(0) == 0)
    def _():
        carry_ref[...] = jnp.zeros_like(carry_ref)

    bsz = u0_ref.shape[0]
    w = 2 * S5_BLOCK_STATE

    def to_time_major(d, u_ref):
        for b in range(bsz):
            ub = u_ref[b].astype(F32)
            for j in range(n_blk):
                tm_ref[d, j, _tm_rows(b, tt), :] = ub[:, j * S5_BLOCK_IN:(j + 1) * S5_BLOCK_IN]

    def project_in(d, j):
        bu_ref[d, :, j * w:(j + 1) * w] = jnp.dot(
            tm_ref[d, j].astype(BF16), bmat_ref[d, j], preferred_element_type=F32)

    def recurrence(d, j):
        re_cols = slice(j * w, j * w + S5_BLOCK_STATE)
        im_cols = slice(j * w + S5_BLOCK_STATE, (j + 1) * w)
        a_re = jnp.broadcast_to(are_ref[d, j:j + 1, :], (SUBLANES, S5_BLOCK_STATE))
        a_im = jnp.broadcast_to(aim_ref[d, j:j + 1, :], (SUBLANES, S5_BLOCK_STATE))
        h_re = carry_ref[d, :, re_cols]
        h_im = carry_ref[d, :, im_cols]
        for k in range(tt):
            t = (tt - 1 - k) if d else k
            rows = slice(t * SUBLANES, (t + 1) * SUBLANES)
            h_re, h_im = (a_re * h_re - a_im * h_im + bu_ref[d, rows, re_cols],
                          a_re * h_im + a_im * h_re + bu_ref[d, rows, im_cols])
            bu_ref[d, rows, re_cols] = h_re
            bu_ref[d, rows, im_cols] = h_im
        carry_ref[d, :, re_cols] = h_re
        carry_ref[d, :, im_cols] = h_im

    def project_out(d, j):
        tm_ref[d, j] = jnp.dot(bu_ref[d, :, j * w:(j + 1) * w].astype(BF16), cmat_ref[d, j],
                               preferred_element_type=F32)

    def to_batch_major(d, y_ref):
        for b in range(bsz):
            y_ref[b] = jnp.concatenate(
                [tm_ref[d, j, _tm_rows(b, tt), :] for j in range(n_blk)], axis=-1).astype(BF16)

    to_time_major(0, u0_ref)
    to_time_major(1, u1_ref)
    for j in range(n_blk):
        project_in(0, j)
    for j in range(n_blk):
        recurrence(0, j)
        project_in(1, j)
    for j in range(n_blk):
        project_out(0, j)
        recurrence(1, j)
    to_batch_major(0, y0_ref)
    for j in range(n_blk):
        project_out(1, j)
    to_batch_major(1, y1_ref)


def _s5_scan(u, bmat, a_re, a_im, cmat):
    bsz, seq, width = u.shape
    tt = S5_TIME_TILE
    rows = tt * bsz
    nc = seq // tt
    n_blk = width // S5_BLOCK_IN
    state_w = n_blk * 2 * S5_BLOCK_STATE
    kern = functools.partial(_s5_kernel, tt=tt, n_blk=n_blk)
    fwd = pl.BlockSpec((bsz, tt, width), lambda i: (0, i, 0))
    bwd = pl.BlockSpec((bsz, tt, width), lambda i: (0, nc - 1 - i, 0))
    return pl.pallas_call(
        kern,
        grid=(nc,),
        in_specs=[fwd, bwd, _const_spec(bmat.shape), _const_spec(a_re.shape), _const_spec(a_im.shape),
                  _const_spec(cmat.shape)],
        out_specs=[fwd, bwd],
        out_shape=[jax.ShapeDtypeStruct((bsz, seq, width), BF16)] * 2,
        scratch_shapes=[
            pltpu.VMEM((2, n_blk, rows, LANES), F32),
            pltpu.VMEM((2, rows, state_w), F32),
            pltpu.VMEM((2, SUBLANES, state_w), F32),
        ],
        compiler_params=_cparams(("arbitrary",)),
        name="s5_scan",
    )(u, u, bmat, a_re, a_im, cmat)


def _attn_kernel(q_ref, k_ref, v_ref, o_ref, kt_ref, vext_ref, s_ref, *, seq, tq, tk, n_heads):
    n_q = seq // tq
    n_kv = seq // tk
    heads_per_kv = n_heads // N_KV_HEADS

    def prep(c, carry):
        rows = pl.ds(pl.multiple_of(c * tk, tk), tk)
        kf = k_ref[0, rows, :].astype(F32)
        vf = v_ref[0, rows, :].astype(F32)
        lane = lax.broadcasted_iota(jnp.int32, kf.shape, 1)
        low = lane < HEAD_DIM
        k_sw = pltpu.roll(kf, HEAD_DIM, axis=1)
        v_sw = pltpu.roll(vf, HEAD_DIM, axis=1)
        kt_ref[0, 0, c] = jnp.where(low, kf, 0.0).T.astype(BF16)
        kt_ref[0, 1, c] = jnp.where(low, 0.0, k_sw).T.astype(BF16)
        kt_ref[1, 0, c] = jnp.where(low, k_sw, 0.0).T.astype(BF16)
        kt_ref[1, 1, c] = jnp.where(low, 0.0, kf).T.astype(BF16)
        vext_ref[0, 0, c] = jnp.where(low, vf, 1.0).astype(BF16)
        vext_ref[0, 1, c] = jnp.where(low, 1.0, v_sw).astype(BF16)
        vext_ref[1, 0, c] = jnp.where(low, v_sw, 1.0).astype(BF16)
        vext_ref[1, 1, c] = jnp.where(low, 1.0, vf).astype(BF16)
        return carry

    lax.fori_loop(0, n_kv, prep, 0)

    def head_ids(h):
        return h // heads_per_kv, h % 2, h // 2

    def scores_chunk(qp, h, c, m_part):
        kvh, e, _ = head_ids(h)
        s = jnp.dot(qp, kt_ref[kvh, e, c], preferred_element_type=F32)
        s_ref[h % 2, c] = s
        for j in range(tk // LANES):
            m_part = jnp.maximum(m_part, s[:, j * LANES:(j + 1) * LANES])
        return m_part

    def pv_chunk(h, c, m_full, acc):
        kvh, e, _ = head_ids(h)
        s = s_ref[h % 2, c]
        p = jnp.concatenate([jnp.exp2(s[:, j * LANES:(j + 1) * LANES] - m_full)
                             for j in range(tk // LANES)], axis=-1).astype(BF16)
        return acc + jnp.dot(p, vext_ref[kvh, e, c], preferred_element_type=F32)

    def row_max(m_part):
        return jnp.broadcast_to(jnp.max(m_part, axis=-1, keepdims=True), (tq, LANES))

    def q_pair(qt, h):
        pair = head_ids(h)[2]
        return q_ref[0, pl.ds(pl.multiple_of(qt * tq, tq), tq), pair * LANES:(pair + 1) * LANES]

    neg = jnp.full((tq, LANES), -jnp.inf, F32)
    zero = jnp.zeros((tq, LANES), F32)

    qp0 = q_pair(0, 0)
    m0 = row_max(lax.fori_loop(0, n_kv, lambda c, m: scores_chunk(qp0, 0, c, m), neg))

    def q_tile(qt, m_head0):
        m_cur = m_head0
        r_even = None
        for h in range(n_heads):
            h_next = (h + 1) % n_heads
            qt_next = qt if h_next else jnp.minimum(qt + 1, n_q - 1)
            qp = q_pair(qt_next, h_next)
            m_part, r = neg, zero
            for c in range(n_kv):
                m_part = scores_chunk(qp, h_next, c, m_part)
                r = pv_chunk(h, c, m_cur, r)
            m_cur = row_max(m_part)
            if h % 2 == 0:
                r_even = r
            else:
                pair = h // 2
                lane = lax.broadcasted_iota(jnp.int32, r.shape, 1)
                o_pair = jnp.where(lane < HEAD_DIM,
                                   r_even / pltpu.roll(r_even, HEAD_DIM, axis=1),
                                   r / pltpu.roll(r, HEAD_DIM, axis=1))
                o_ref[0, pl.ds(pl.multiple_of(qt * tq, tq), tq), pair * LANES:(pair + 1) * LANES] = (
                    o_pair.astype(BF16))
        return m_cur

    lax.fori_loop(0, n_q, q_tile, m0)


def _attention(q, k, v):
    bsz, seq, q_w = q.shape
    kv_w = k.shape[-1]
    tq, tk = ATTN_Q_TILE, ATTN_KV_TILE
    n_heads = q_w // HEAD_DIM
    kern = functools.partial(_attn_kernel, seq=seq, tq=tq, tk=tk, n_heads=n_heads)

    def row(width):
        return pl.BlockSpec((1, seq, width), lambda b: (b, 0, 0))

    return pl.pallas_call(
        kern,
        grid=(bsz,),
        in_specs=[row(q_w), row(kv_w), row(kv_w)],
        out_specs=row(q_w),
        out_shape=jax.ShapeDtypeStruct((bsz, seq, q_w), BF16),
        scratch_shapes=[
            pltpu.VMEM((N_KV_HEADS, 2, seq // tk, LANES, tk), BF16),
            pltpu.VMEM((N_KV_HEADS, 2, seq // tk, tk, LANES), BF16),
            pltpu.VMEM((2, seq // tk, tq, tk), F32),
        ],
        compiler_params=_cparams(("arbitrary",)),
        name="attention",
    )(q, k, v)


def _residual_mlp_tail(x, mix, mod, nw2_ref, w1_ref, w2_ref, hmid_ref, fw_ref, o_ref, final):
    h1 = x + mod[2:3] * mix
    u = _norm_mod(h1, nw2_ref[...], mod[4:5], mod[3:4]).astype(BF16)
    d_ff = w1_ref.shape[1]
    for f in range(d_ff // FF_TILE):
        cols = slice(f * FF_TILE, (f + 1) * FF_TILE)
        a = jnp.maximum(jnp.dot(u, w1_ref[:, cols], preferred_element_type=F32), 0.0)
        hmid_ref[:, cols] = (a * a).astype(BF16)
    ff = jnp.dot(hmid_ref[...], w2_ref[...], preferred_element_type=F32)
    h2 = h1 + mod[5:6] * ff
    if final:
        rs = lax.rsqrt(jnp.mean(h2 * h2, axis=-1, keepdims=True) + EPS)
        h2 = h2 * rs * fw_ref[...]
    o_ref[0] = h2


def _hyb_out_kernel(zs5_ref, yf_ref, yb_ref, att_ref, h_ref, mod_ref, d_ref, gluw_ref, glub_ref,
                    wout_ref, nw2_ref, w1_ref, w2_ref, fw_ref, o_ref, hmid_ref, *, s5_w, final):
    mod = mod_ref[0]
    y = zs5_ref[0].astype(F32) * d_ref[...] + yf_ref[0].astype(F32) + yb_ref[0].astype(F32)
    g = _gelu(y)
    gate = _sigmoid(jnp.dot(g.astype(BF16), gluw_ref[...], preferred_element_type=F32) + glub_ref[...])
    ys5 = (g * gate).astype(BF16)
    mix = (jnp.dot(ys5, wout_ref[:s5_w, :], preferred_element_type=F32)
           + jnp.dot(att_ref[0], wout_ref[s5_w:, :], preferred_element_type=F32))
    _residual_mlp_tail(h_ref[0], mix, mod, nw2_ref, w1_ref, w2_ref, hmid_ref, fw_ref, o_ref, final)


def _hyb_out(zs5, yf, yb, att, h, mod, d_row, glu_w, glu_b, w_out, nw2, w1, w2, fw, final):
    bsz, seq, d = h.shape
    ts = TOKEN_TILE
    s5_w = glu_w.shape[0]
    d_ff = w1.shape[1]

    def tok(width):
        return pl.BlockSpec((1, ts, width), lambda b, s: (b, s, 0))

    kern = functools.partial(_hyb_out_kernel, s5_w=s5_w, final=final)
    return pl.pallas_call(
        kern,
        grid=(bsz, seq // ts),
        in_specs=[
            tok(s5_w), tok(s5_w), tok(s5_w), tok(att.shape[-1]), tok(d),
            pl.BlockSpec((1, 6, d), lambda b, s: (b, 0, 0)),
            _const_spec((1, s5_w)),
            _const_spec(glu_w.shape),
            _const_spec((1, s5_w)),
            _const_spec(w_out.shape),
            _const_spec((1, d)),
            _const_spec(w1.shape),
            _const_spec(w2.shape),
            _const_spec((1, d)),
        ],
        out_specs=tok(d),
        out_shape=jax.ShapeDtypeStruct((bsz, seq, d), F32),
        scratch_shapes=[pltpu.VMEM((ts, d_ff), BF16)],
        compiler_params=_cparams(("arbitrary", "arbitrary")),
        name="hyb_out_mlp",
    )(zs5, yf, yb, att, h, mod, d_row, glu_w, glu_b, w_out, nw2, w1, w2, fw)


def _rec_in_kernel(h_ref, mod_ref, nw_ref, w_ref, gate_ref, xr_ref, *, width):
    mod = mod_ref[0]
    u = _norm_mod(h_ref[0], nw_ref[...], mod[1:2], mod[0:1]).astype(BF16)
    gate_ref[0] = jnp.dot(u, w_ref[:, :width], preferred_element_type=F32).astype(BF16)
    xr_ref[0] = jnp.dot(u, w_ref[:, width:], preferred_element_type=F32)


def _rec_in(h, mod, nw, w_in):
    bsz, seq, d = h.shape
    ts = TOKEN_TILE
    width = w_in.shape[1] // 2
    kern = functools.partial(_rec_in_kernel, width=width)

    def tok(w):
        return pl.BlockSpec((1, ts, w), lambda b, s: (b, s, 0))

    return pl.pallas_call(
        kern,
        grid=(bsz, seq // ts),
        in_specs=[
            tok(d),
            pl.BlockSpec((1, 6, d), lambda b, s: (b, 0, 0)),
            _const_spec((1, d)),
            _const_spec(w_in.shape),
        ],
        out_specs=[tok(width), tok(width)],
        out_shape=[
            jax.ShapeDtypeStruct((bsz, seq, width), BF16),
            jax.ShapeDtypeStruct((bsz, seq, width), F32),
        ],
        compiler_params=_cparams(("arbitrary", "arbitrary")),
        name="rec_in",
    )(h, mod, nw, w_in)


def _quarter_one_minus_exp(x, log2_a, a):
    series = (-0.5 * LN_2) * log2_a * (1.0 + x * (1 / 2 + x * (1 / 6)))
    return jnp.where(x > -1 / 32, series, 0.25 - 0.25 * (a * a))


def _lru_direction(main_ref, prev_ref, next_ref, at_start, at_end, convw_ref, convb_ref,
                   wr_ref, wi_ref, rab_ref, ixb_ref, nsp_ref, out_ref,
                   xext_ref, a_ref, hs_ref, carry_ref, *, tt, reverse):
    bsz = main_ref.shape[0]
    rows = tt * SUBLANES
    halo_t = prev_ref.shape[1]
    halo = halo_t * SUBLANES
    width = main_ref.shape[2]
    n_lane_blk = width // LANES
    for b in range(bsz):
        xp = jnp.where(at_start, 0.0, prev_ref[b])
        xm = main_ref[b]
        xn = jnp.where(at_end, 0.0, next_ref[b])
        for l in range(n_lane_blk):
            lanes = slice(l * LANES, (l + 1) * LANES)
            xext_ref[l, _tm_rows(b, halo_t), :] = xp[:, lanes]
            xext_ref[l, _tm_rows(b, tt, halo), :] = xm[:, lanes]
            xext_ref[l, _tm_rows(b, halo_t, halo + rows), :] = xn[:, lanes]
    blk = LRU_HEADS_PER_BLOCK * (width // LRU_HEADS)
    lane_per_blk = blk // LANES
    for j in range(width // blk):
        cols = slice(j * blk, (j + 1) * blk)
        for r0 in range(0, rows, LRU_GATE_ROWS):
            parts = []
            for l in range(j * lane_per_blk, (j + 1) * lane_per_blk):
                lanes = slice(l * LANES, (l + 1) * LANES)
                acc = convb_ref[:, lanes]
                for tap in range(CONV_WIDTH):
                    first = halo + (tap - CONV_LEFT) * SUBLANES + r0
                    acc = acc + convw_ref[tap:tap + 1, lanes] * xext_ref[l, first:first + LRU_GATE_ROWS, :]
                parts.append(acc)
            xc = jnp.concatenate(parts, axis=-1)
            xcb = xc.astype(BF16)
            t_r = jnp.tanh(jnp.dot(xcb, wr_ref[j], preferred_element_type=F32) + rab_ref[:, cols])
            t_i = jnp.tanh(jnp.dot(xcb, wi_ref[j], preferred_element_type=F32) + ixb_ref[:, cols])
            log2_a = nsp_ref[:, cols] * (t_r + 1.0)
            a = jnp.exp2(log2_a)
            om = _quarter_one_minus_exp((2.0 * LN_2) * log2_a, log2_a, a)
            bt = (om * lax.rsqrt(jnp.maximum(om, SQRT_FLOOR))) * ((t_i + 1.0) * xc)
            for k in range(lane_per_blk):
                l = j * lane_per_blk + k
                a_ref[l, r0:r0 + LRU_GATE_ROWS, :] = a[:, k * LANES:(k + 1) * LANES]
                hs_ref[l, r0:r0 + LRU_GATE_ROWS, :] = bt[:, k * LANES:(k + 1) * LANES]

    def body(k, h):
        t = (tt - 1 - k) if reverse else k
        r_t = pl.ds(pl.multiple_of(t * SUBLANES, SUBLANES), SUBLANES)
        new = []
        for l in range(n_lane_blk):
            h_l = a_ref[l, r_t, :] * h[l] + hs_ref[l, r_t, :]
            hs_ref[l, r_t, :] = h_l
            new.append(h_l)
        return tuple(new)

    h0 = tuple(carry_ref[:, l * LANES:(l + 1) * LANES] for l in range(n_lane_blk))
    h_end = lax.fori_loop(0, tt, body, h0, unroll=8)
    for l in range(n_lane_blk):
        carry_ref[:, l * LANES:(l + 1) * LANES] = h_end[l]
    for b in range(bsz):
        out_ref[b] = jnp.concatenate(
            [hs_ref[l, _tm_rows(b, tt), :] for l in range(n_lane_blk)], axis=-1).astype(BF16)


def _lru_kernel(fm_ref, fp_ref, fn_ref, bm_ref, bp_ref, bn_ref, convw_ref, convb_ref,
                wr_ref, wi_ref, rab_ref, ixb_ref, nsp_ref, hf_ref, hb_ref,
                xext_ref, a_ref, hs_ref, carry_ref, *, tt):
    i = pl.program_id(0)
    last = pl.num_programs(0) - 1

    @pl.when(i == 0)
    def _():
        carry_ref[...] = jnp.zeros_like(carry_ref)

    _lru_direction(fm_ref, fp_ref, fn_ref, i == 0, i == last, convw_ref, convb_ref,
                   wr_ref.at[0], wi_ref.at[0], rab_ref.at[0], ixb_ref.at[0], nsp_ref.at[0], hf_ref,
                   xext_ref, a_ref, hs_ref, carry_ref.at[0], tt=tt, reverse=False)
    _lru_direction(bm_ref, bp_ref, bn_ref, i == last, i == 0, convw_ref, convb_ref,
                   wr_ref.at[1], wi_ref.at[1], rab_ref.at[1], ixb_ref.at[1], nsp_ref.at[1], hb_ref,
                   xext_ref, a_ref, hs_ref, carry_ref.at[1], tt=tt, reverse=True)


def _lru_scan(xr, conv_w, conv_b, wr, wi, ra_b, ix_b, nsp):
    bsz, seq, width = xr.shape
    tt = LRU_TIME_TILE
    rows = tt * bsz
    nc = seq // tt
    halo_t = SUBLANES
    per = tt // halo_t
    n_halo = seq // halo_t

    def main_f(i):
        return (0, i, 0)

    def prev_f(i):
        return (0, jnp.maximum(i * per - 1, 0), 0)

    def next_f(i):
        return (0, jnp.minimum((i + 1) * per, n_halo - 1), 0)

    def rev(f):
        return lambda i: f(nc - 1 - i)

    main_blk = (bsz, tt, width)
    halo_blk = (bsz, halo_t, width)
    kern = functools.partial(_lru_kernel, tt=tt)
    return pl.pallas_call(
        kern,
        grid=(nc,),
        in_specs=[
            pl.BlockSpec(main_blk, main_f),
            pl.BlockSpec(halo_blk, prev_f),
            pl.BlockSpec(halo_blk, next_f),
            pl.BlockSpec(main_blk, rev(main_f)),
            pl.BlockSpec(halo_blk, rev(prev_f)),
            pl.BlockSpec(halo_blk, rev(next_f)),
            _const_spec(conv_w.shape),
            _const_spec(conv_b.shape),
            _const_spec(wr.shape),
            _const_spec(wi.shape),
            _const_spec(ra_b.shape),
            _const_spec(ix_b.shape),
            _const_spec(nsp.shape),
        ],
        out_specs=[
            pl.BlockSpec(main_blk, main_f),
            pl.BlockSpec(main_blk, rev(main_f)),
        ],
        out_shape=[jax.ShapeDtypeStruct((bsz, seq, width), BF16)] * 2,
        scratch_shapes=[
            pltpu.VMEM((width // LANES, rows + 2 * halo_t * SUBLANES, LANES), F32),
            pltpu.VMEM((width // LANES, rows, LANES), F32),
            pltpu.VMEM((width // LANES, rows, LANES), F32),
            pltpu.VMEM((2, SUBLANES, width), F32),
        ],
        compiler_params=_cparams(("arbitrary",)),
        name="lru_scan",
    )(xr, xr, xr, xr, xr, xr, conv_w, conv_b, wr, wi, ra_b, ix_b, nsp)


def _rec_out_kernel(hf_ref, hb_ref, gate_ref, h_ref, mod_ref, wout_ref, nw2_ref, w1_ref, w2_ref,
                    fw_ref, o_ref, hmid_ref, *, final):
    mod = mod_ref[0]
    y = (hf_ref[0].astype(F32) + hb_ref[0].astype(F32)) * _gelu(gate_ref[0].astype(F32))
    mix = jnp.dot(y.astype(BF16), wout_ref[...], preferred_element_type=F32)
    _residual_mlp_tail(h_ref[0], mix, mod, nw2_ref, w1_ref, w2_ref, hmid_ref, fw_ref, o_ref, final)


def _rec_out(hf, hb, gate, h, mod, w_out, nw2, w1, w2, fw, final):
    bsz, seq, d = h.shape
    ts = TOKEN_TILE
    width = w_out.shape[0]
    d_ff = w1.shape[1]

    def tok(w):
        return pl.BlockSpec((1, ts, w), lambda b, s: (b, s, 0))

    kern = functools.partial(_rec_out_kernel, final=final)
    return pl.pallas_call(
        kern,
        grid=(bsz, seq // ts),
        in_specs=[
            tok(width), tok(width), tok(width), tok(d),
            pl.BlockSpec((1, 6, d), lambda b, s: (b, 0, 0)),
            _const_spec(w_out.shape),
            _const_spec((1, d)),
            _const_spec(w1.shape),
            _const_spec(w2.shape),
            _const_spec((1, d)),
        ],
        out_specs=tok(d),
        out_shape=jax.ShapeDtypeStruct((bsz, seq, d), F32),
        scratch_shapes=[pltpu.VMEM((ts, d_ff), BF16)],
        compiler_params=_cparams(("arbitrary", "arbitrary")),
        name="rec_out_mlp",
    )(hf, hb, gate, h, mod, w_out, nw2, w1, w2, fw)


def _s5_discretize(lam_re, lam_im, log_dt, b_re, b_im, c_re, c_im):
    lr = jnp.minimum(lam_re, S5_MAX_RE)
    li = lam_im
    dt = jnp.exp(log_dt)[..., None]
    mag = jnp.exp(lr * dt)
    abar_re = mag * jnp.cos(li * dt)
    abar_im = mag * jnp.sin(li * dt)
    den = lr * lr + li * li
    nr = abar_re - 1.0
    f_re = (nr * lr + abar_im * li) / den
    f_im = (abar_im * lr - nr * li) / den
    bb_re = f_re[..., None] * b_re - f_im[..., None] * b_im
    bb_im = f_re[..., None] * b_im + f_im[..., None] * b_re
    n_dir, groups, p, c = bb_re.shape
    gpb = S5_GROUPS_PER_BLOCK
    n_blk = groups // gpb
    eye = jnp.eye(gpb, dtype=F32)

    def pack_b(bb):
        bb = bb.reshape(n_dir, n_blk, gpb, p, c)
        return jnp.einsum("djgpc,gh->djgchp", bb, eye).reshape(n_dir, n_blk, gpb * c, gpb * p)

    def pack_c(cc):
        cc = cc.reshape(n_dir, n_blk, gpb, c, p)
        return jnp.einsum("djgcp,gh->djgphc", cc, eye).reshape(n_dir, n_blk, gpb * p, gpb * c)

    bmat = jnp.concatenate([pack_b(bb_re), pack_b(bb_im)], axis=-1).astype(BF16)
    cmat = jnp.concatenate([pack_c(c_re), -pack_c(c_im)], axis=-2).astype(BF16)
    a_re = abar_re.reshape(n_dir, n_blk, gpb * p)
    a_im = abar_im.reshape(n_dir, n_blk, gpb * p)
    return bmat, a_re, a_im, cmat


def _rope_tables(seq):
    rows = seq // GRID_W
    row_idx = jnp.repeat(jnp.arange(rows, dtype=F32), GRID_W)
    col_idx = jnp.tile(jnp.arange(GRID_W, dtype=F32), rows)
    inv_freq = ROPE_BASE ** (-jnp.arange(ROPE_FREQS, dtype=F32) / ROPE_FREQS)
    ang_r = row_idx[:, None] * inv_freq
    ang_c = col_idx[:, None] * inv_freq
    cos = jnp.concatenate([jnp.cos(ang_r)] * 2 + [jnp.cos(ang_c)] * 2, axis=-1)
    sin = jnp.concatenate([-jnp.sin(ang_r), jnp.sin(ang_r), -jnp.sin(ang_c), jnp.sin(ang_c)], axis=-1)
    reps = LANES // HEAD_DIM
    return jnp.tile(cos, (1, reps)), jnp.tile(sin, (1, reps))


def _block_diag_heads(w):
    n_dir, heads, i, j = w.shape
    hpb = LRU_HEADS_PER_BLOCK
    eye = jnp.eye(hpb, dtype=F32)
    w = w.reshape(n_dir, heads // hpb, hpb, i, j)
    return jnp.einsum("dnhij,hg->dnhigj", w, eye).reshape(n_dir, heads // hpb, hpb * i, hpb * j).astype(BF16)


def kernel(x, c, norm_w, ada_w, ada_b, mlp_w1, mlp_w2, final_norm_w, hyb_w_in, s5_lam_re, s5_lam_im,
           s5_log_dt, s5_b_re, s5_b_im, s5_c_re, s5_c_im, s5_d, s5_glu_w, s5_glu_b, attn_q_norm,
           attn_k_norm, hyb_w_out, rec_w_in, rec_conv_w, rec_conv_b, rec_ra_w, rec_ra_b, rec_ix_w,
           rec_ix_b, rec_lam, rec_w_out):
    bsz, seq, d = x.shape
    depth = norm_w.shape[0]
    assert bsz == SUBLANES, "the time-major recurrent layout needs one sublane tile of batch rows"
    s5_w = s5_d.shape[-1]
    kv_w = N_KV_HEADS * HEAD_DIM
    q_w = hyb_w_in.shape[-1] - s5_w - 2 * kv_w

    mod_all = _ada_mod(c, ada_w, ada_b).reshape(depth, bsz, 6, d)
    cos, sin = _rope_tables(seq)
    reps = LANES // HEAD_DIM
    head_id = jnp.arange(LANES) // HEAD_DIM
    g_ones = (head_id[:, None] == head_id[None, :]).astype(BF16)
    fw = final_norm_w.reshape(1, d)

    h = x
    for layer in range(depth):
        mod = mod_all[layer]
        nw1 = norm_w[layer, 0].reshape(1, d)
        nw2 = norm_w[layer, 1].reshape(1, d)
        w1 = mlp_w1[layer].astype(BF16)
        w2 = mlp_w2[layer].astype(BF16)
        final = layer == depth - 1
        if layer % 2 == 0:
            e = layer // 2
            qn_row = jnp.tile(attn_q_norm[e] * (HEAD_DIM ** -0.5 * LOG2_E), reps).reshape(1, LANES)
            kn_row = jnp.tile(attn_k_norm[e], reps).reshape(1, LANES)
            zs5, q, k, v = _hyb_in(h, mod, nw1, hyb_w_in[e].astype(BF16), g_ones, qn_row, kn_row,
                                   cos, sin, s5_w, q_w, kv_w)
            bmat, a_re, a_im, cmat = _s5_discretize(s5_lam_re[e], s5_lam_im[e], s5_log_dt[e],
                                                    s5_b_re[e], s5_b_im[e], s5_c_re[e], s5_c_im[e])
            yf, yb = _s5_scan(zs5, bmat, a_re, a_im, cmat)
            att = _attention(q, k, v)
            h = _hyb_out(zs5, yf, yb, att, h, mod, s5_d[e].reshape(1, s5_w), s5_glu_w[e].astype(BF16),
                         s5_glu_b[e].reshape(1, s5_w), hyb_w_out[e].astype(BF16), nw2, w1, w2, fw, final)
        else:
            o = layer // 2
            width = rec_w_out.shape[1]
            gate, xr = _rec_in(h, mod, nw1, rec_w_in[o].astype(BF16))
            nsp = ((-0.5 * LRU_C * LOG2_E) * jax.nn.softplus(-rec_lam[o])).reshape(2, 1, width)
            hf, hb = _lru_scan(xr, rec_conv_w[o], rec_conv_b[o].reshape(1, width),
                               _block_diag_heads(0.5 * rec_ra_w[o]), _block_diag_heads(0.5 * rec_ix_w[o]),
                               (0.5 * rec_ra_b[o]).reshape(2, 1, width), (0.5 * rec_ix_b[o]).reshape(2, 1, width),
                               nsp)
            h = _rec_out(hf, hb, gate, h, mod, rec_w_out[o].astype(BF16), nw2, w1, w2, fw, final)
    return h
```

```python
import functools
import math

import jax
import jax.numpy as jnp
from jax import lax
from jax.experimental import pallas as pl
from jax.experimental.pallas import tpu as pltpu

F32 = jnp.float32
BF16 = jnp.bfloat16

EPS = 1e-6
GRID_W = 64

S5_GROUP = 16
S5_STATE = 64
S5_MAX_RE = -1e-4
S5_GROUPS_PER_BLOCK = 8
S5_BLOCK_IN = S5_GROUPS_PER_BLOCK * S5_GROUP
S5_BLOCK_STATE = S5_GROUPS_PER_BLOCK * S5_STATE

HEAD_DIM = 64
N_KV_HEADS = 2
ROPE_FREQS = HEAD_DIM // 4
ROPE_BASE = 10000.0

LRU_HEADS = 16
LRU_HEADS_PER_BLOCK = 4
CONV_WIDTH = 4
CONV_LEFT = 2
LRU_C = 8.0

SUBLANES = 8
LANES = 128

TOKEN_TILE = 512
ATTN_Q_TILE = 512
ATTN_KV_TILE = 512
S5_TIME_TILE = 64
LRU_TIME_TILE = 32
LRU_GATE_ROWS = 64
FF_TILE = 512
VMEM_LIMIT = 56 * 1024 * 1024

SQRT_FLOOR = 1e-37
LOG2_E = math.log2(math.e)
LN_2 = math.log(2.0)


def _cparams(sem):
    return pltpu.CompilerParams(dimension_semantics=sem, vmem_limit_bytes=VMEM_LIMIT)


def _const_spec(shape):
    nd = len(shape)
    return pl.BlockSpec(shape, lambda *_: (0,) * nd, pipeline_mode=pl.Buffered(1))


def _norm_mod(x, w_row, sc_row, sh_row):
    rs = lax.rsqrt(jnp.mean(x * x, axis=-1, keepdims=True) + EPS)
    return x * rs * (w_row * (1.0 + sc_row)) + sh_row


def _sigmoid(x):
    return 1.0 / (1.0 + jnp.exp(-x))


def _gelu(x):
    return jax.nn.gelu(x, approximate=True)


def _lane_block_spec(width, ts):
    return pl.BlockSpec((1, width // LANES, ts, LANES), lambda b, s: (b, 0, s, 0))


def _cat_lane_blocks(ref):
    return jnp.concatenate([ref[0, j] for j in range(ref.shape[1])], axis=-1)


def _tm_rows(b, tt, first_row=0):
    return pl.ds(first_row + b, tt, stride=SUBLANES)


def _ada_kernel(c_ref, w_ref, b_ref, o_ref):
    c = c_ref[...]
    ca = (c * _sigmoid(c)).astype(BF16)
    o_ref[0] = jnp.dot(ca, w_ref[0].astype(BF16), preferred_element_type=F32) + b_ref[0]


def _ada_mod(c, ada_w, ada_b):
    depth, d, n = ada_w.shape
    bsz = c.shape[0]
    tn = 1024
    return pl.pallas_call(
        _ada_kernel,
        grid=(depth, n // tn),
        in_specs=[
            pl.BlockSpec((bsz, d), lambda l, j: (0, 0)),
            pl.BlockSpec((1, d, tn), lambda l, j: (l, 0, j)),
            pl.BlockSpec((1, 1, tn), lambda l, j: (l, 0, j)),
        ],
        out_specs=pl.BlockSpec((1, bsz, tn), lambda l, j: (l, 0, j)),
        out_shape=jax.ShapeDtypeStruct((depth, bsz, n), F32),
        compiler_params=_cparams(("arbitrary", "arbitrary")),
        name="ada_mod",
    )(c, ada_w, ada_b.reshape(depth, 1, n))


def _head_norm_rope(z, g_ones, nw_row, cos, sin_signed):
    ss = jnp.dot((z * z).astype(BF16), g_ones, preferred_element_type=F32)
    zn = z * lax.rsqrt(ss * (1.0 / HEAD_DIM) + EPS) * nw_row
    lane = lax.broadcasted_iota(jnp.int32, zn.shape, 1)
    first_half = (lane % (2 * ROPE_FREQS)) < ROPE_FREQS
    partner = jnp.where(first_half,
                        pltpu.roll(zn, LANES - ROPE_FREQS, axis=1),
                        pltpu.roll(zn, ROPE_FREQS, axis=1))
    return zn * cos + partner * sin_signed


def _hyb_in_kernel(h_ref, mod_ref, nw_ref, w_ref, gones_ref, qn_ref, kn_ref, cos_ref, sin_ref,
                   zs5_ref, q_ref, k_ref, v_ref, *, s5_w, q_w, kv_w):
    x = h_ref[0]
    mod = mod_ref[0]
    u = _norm_mod(x, nw_ref[...], mod[1:2], mod[0:1]).astype(BF16)
    z = jnp.dot(u, w_ref[...], preferred_element_type=F32)
    for j in range(s5_w // LANES):
        zs5_ref[0, j] = z[:, j * LANES:(j + 1) * LANES]
    cos = cos_ref[...]
    sin = sin_ref[...]
    g_ones = gones_ref[...]
    for j in range(q_w // LANES):
        zq = z[:, s5_w + j * LANES: s5_w + (j + 1) * LANES]
        q_ref[0, :, j * LANES:(j + 1) * LANES] = _head_norm_rope(
            zq, g_ones, qn_ref[...], cos, sin).astype(BF16)
    for j in range(kv_w // LANES):
        zk = z[:, s5_w + q_w + j * LANES: s5_w + q_w + (j + 1) * LANES]
        k_ref[0, :, j * LANES:(j + 1) * LANES] = _head_norm_rope(
            zk, g_ones, kn_ref[...], cos, sin).astype(BF16)
    v_ref[0] = z[:, s5_w + q_w + kv_w:].astype(BF16)


def _hyb_in(h, mod, nw, w_in, g_ones, qn_row, kn_row, cos, sin, s5_w, q_w, kv_w):
    bsz, seq, d = h.shape
    ts = TOKEN_TILE
    n = w_in.shape[1]
    kern = functools.partial(_hyb_in_kernel, s5_w=s5_w, q_w=q_w, kv_w=kv_w)

    def tok(width):
        return pl.BlockSpec((1, ts, width), lambda b, s: (b, s, 0))

    return pl.pallas_call(
        kern,
        grid=(bsz, seq // ts),
        in_specs=[
            tok(d),
            pl.BlockSpec((1, 6, d), lambda b, s: (b, 0, 0)),
            _const_spec((1, d)),
            _const_spec((d, n)),
            _const_spec((LANES, LANES)),
            _const_spec((1, LANES)),
            _const_spec((1, LANES)),
            pl.BlockSpec((ts, LANES), lambda b, s: (s, 0)),
            pl.BlockSpec((ts, LANES), lambda b, s: (s, 0)),
        ],
        out_specs=[_lane_block_spec(s5_w, ts), tok(q_w), tok(kv_w), tok(kv_w)],
        out_shape=[
            jax.ShapeDtypeStruct((bsz, s5_w // LANES, seq, LANES), F32),
            jax.ShapeDtypeStruct((bsz, seq, q_w), BF16),
            jax.ShapeDtypeStruct((bsz, seq, kv_w), BF16),
            jax.ShapeDtypeStruct((bsz, seq, kv_w), BF16),
        ],
        compiler_params=_cparams(("arbitrary", "arbitrary")),
        name="hyb_in",
    )(h, mod, nw, w_in, g_ones, qn_row, kn_row, cos, sin)


def _s5_kernel(u0_ref, u1_ref, bpair_ref, are_ref, aim_ref, cpair_ref, dmat_ref, y0_ref, y1_ref,
               tm_ref, ys_ref, bu_ref, carry_ref, *, tt, n_blk):
    bsz = u0_ref.shape[0]
    w = 2 * S5_BLOCK_STATE
    pairs = tt // 2
    prow = pairs * SUBLANES
    even, odd = 0, 1
    base = {(0, even): 0, (0, odd): SUBLANES, (1, even): 0, (1, odd): 0}

    @pl.when(pl.program_id(0) == 0)
    def _():
        carry_ref[...] = jnp.zeros_like(carry_ref)
        tm_ref[0, odd, :, 0:SUBLANES, :] = jnp.zeros((n_blk, SUBLANES, LANES), F32)
        tm_ref[1, even, :, prow:prow + SUBLANES, :] = jnp.zeros((n_blk, SUBLANES, LANES), F32)

    def to_time_major(d, u_ref):
        for b in range(bsz):
            for j in range(n_blk):
                for parity in (even, odd):
                    tm_ref[d, parity, j, _tm_rows(b, pairs, base[d, parity]), :] = (
                        u_ref[b, j, pl.ds(parity, pairs, stride=2), :])

    def pair_lhs(d, j):
        if d == 0:
            first, second = tm_ref[0, odd, j, 0:prow, :], tm_ref[0, even, j, 0:prow, :]
        else:
            first, second = tm_ref[1, odd, j, 0:prow, :], tm_ref[1, even, j, SUBLANES:SUBLANES + prow, :]
        return jnp.concatenate([first, second], axis=-1).astype(BF16)

    def project_in(d, j):
        bu_ref[d, :, j * w:(j + 1) * w] = jnp.dot(pair_lhs(d, j), bpair_ref[d, j], preferred_element_type=F32)

    def keep_neighbour_tile(d):
        for j in range(n_blk):
            if d == 0:
                tm_ref[0, odd, j, 0:SUBLANES, :] = tm_ref[0, odd, j, prow:prow + SUBLANES, :]
            else:
                tm_ref[1, even, j, prow:prow + SUBLANES, :] = tm_ref[1, even, j, 0:SUBLANES, :]

    def recurrence(d, j):
        re_cols = slice(j * w, j * w + S5_BLOCK_STATE)
        im_cols = slice(j * w + S5_BLOCK_STATE, (j + 1) * w)
        a_re = jnp.broadcast_to(are_ref[d, j:j + 1, :], (SUBLANES, S5_BLOCK_STATE))
        a_im = jnp.broadcast_to(aim_ref[d, j:j + 1, :], (SUBLANES, S5_BLOCK_STATE))
        h_re = carry_ref[d, :, re_cols]
        h_im = carry_ref[d, :, im_cols]
        for k in range(pairs):
            m = (pairs - 1 - k) if d else k
            rows = slice(m * SUBLANES, (m + 1) * SUBLANES)
            h_re, h_im = (a_re * h_re - a_im * h_im + bu_ref[d, rows, re_cols],
                          a_re * h_im + a_im * h_re + bu_ref[d, rows, im_cols])
            bu_ref[d, rows, re_cols] = h_re
            bu_ref[d, rows, im_cols] = h_im
        carry_ref[d, :, re_cols] = h_re
        carry_ref[d, :, im_cols] = h_im

    def project_out(d, j):
        direct_parity = odd if d == 0 else even
        first = base[d, direct_parity]
        u_direct = tm_ref[d, direct_parity, j, first:first + prow, :].astype(BF16)
        y = (jnp.dot(bu_ref[d, :, j * w:(j + 1) * w].astype(BF16), cpair_ref[d, j], preferred_element_type=F32)
             + jnp.dot(u_direct, dmat_ref[d, j], preferred_element_type=F32))
        ys_ref[d, even, j] = y[:, :LANES]
        ys_ref[d, odd, j] = y[:, LANES:]

    def to_batch_major(d, y_ref):
        for b in range(bsz):
            for j in range(n_blk):
                for parity in (even, odd):
                    y_ref[b, j, pl.ds(parity, pairs, stride=2), :] = ys_ref[d, parity, j, _tm_rows(b, pairs), :]

    to_time_major(0, u0_ref)
    to_time_major(1, u1_ref)
    for j in range(n_blk):
        project_in(0, j)
    for j in range(n_blk):
        recurrence(0, j)
        project_in(1, j)
    for j in range(n_blk):
        project_out(0, j)
        recurrence(1, j)
    keep_neighbour_tile(0)
    to_batch_major(0, y0_ref)
    for j in range(n_blk):
        project_out(1, j)
    keep_neighbour_tile(1)
    to_batch_major(1, y1_ref)


def _s5_scan(u, bpair, a_re, a_im, cpair, dmat):
    bsz, n_blk, seq, _ = u.shape
    tt = S5_TIME_TILE
    prow = (tt // 2) * bsz
    nc = seq // tt
    state_w = n_blk * 2 * S5_BLOCK_STATE
    kern = functools.partial(_s5_kernel, tt=tt, n_blk=n_blk)
    fwd = pl.BlockSpec((bsz, n_blk, tt, LANES), lambda i: (0, 0, i, 0))
    bwd = pl.BlockSpec((bsz, n_blk, tt, LANES), lambda i: (0, 0, nc - 1 - i, 0))
    return pl.pallas_call(
        kern,
        grid=(nc,),
        in_specs=[fwd, bwd, _const_spec(bpair.shape), _const_spec(a_re.shape), _const_spec(a_im.shape),
                  _const_spec(cpair.shape), _const_spec(dmat.shape)],
        out_specs=[fwd, bwd],
        out_shape=[jax.ShapeDtypeStruct(u.shape, F32)] * 2,
        scratch_shapes=[
            pltpu.VMEM((2, 2, n_blk, prow + SUBLANES, LANES), F32),
            pltpu.VMEM((2, 2, n_blk, prow, LANES), F32),
            pltpu.VMEM((2, prow, state_w), F32),
            pltpu.VMEM((2, SUBLANES, state_w), F32),
        ],
        compiler_params=_cparams(("arbitrary",)),
        name="s5_scan",
    )(u, u, bpair, a_re, a_im, cpair, dmat)


def _attn_kernel(q_ref, k_ref, v_ref, o_ref, kt_ref, vext_ref, s_ref, *, seq, tq, tk, n_heads):
    n_q = seq // tq
    n_kv = seq // tk
    heads_per_kv = n_heads // N_KV_HEADS

    def prep(c, carry):
        rows = pl.ds(pl.multiple_of(c * tk, tk), tk)
        kf = k_ref[0, rows, :].astype(F32)
        vf = v_ref[0, rows, :].astype(F32)
        lane = lax.broadcasted_iota(jnp.int32, kf.shape, 1)
        low = lane < HEAD_DIM
        k_sw = pltpu.roll(kf, HEAD_DIM, axis=1)
        v_sw = pltpu.roll(vf, HEAD_DIM, axis=1)
        kt_ref[0, 0, c] = jnp.where(low, kf, 0.0).T.astype(BF16)
        kt_ref[0, 1, c] = jnp.where(low, 0.0, k_sw).T.astype(BF16)
        kt_ref[1, 0, c] = jnp.where(low, k_sw, 0.0).T.astype(BF16)
        kt_ref[1, 1, c] = jnp.where(low, 0.0, kf).T.astype(BF16)
        vext_ref[0, 0, c] = jnp.where(low, vf, 1.0).astype(BF16)
        vext_ref[0, 1, c] = jnp.where(low, 1.0, v_sw).astype(BF16)
        vext_ref[1, 0, c] = jnp.where(low, v_sw, 1.0).astype(BF16)
        vext_ref[1, 1, c] = jnp.where(low, 1.0, vf).astype(BF16)
        return carry

    lax.fori_loop(0, n_kv, prep, 0)

    def head_ids(h):
        return h // heads_per_kv, h % 2, h // 2

    def scores_chunk(qp, h, c, m_part):
        kvh, e, _ = head_ids(h)
        s = jnp.dot(qp, kt_ref[kvh, e, c], preferred_element_type=F32)
        s_ref[h % 2, c] = s
        for j in range(tk // LANES):
            m_part = jnp.maximum(m_part, s[:, j * LANES:(j + 1) * LANES])
        return m_part

    def pv_chunk(h, c, m_full, acc):
        kvh, e, _ = head_ids(h)
        s = s_ref[h % 2, c]
        p = jnp.concatenate([jnp.exp2(s[:, j * LANES:(j + 1) * LANES] - m_full)
                             for j in range(tk // LANES)], axis=-1).astype(BF16)
        return acc + jnp.dot(p, vext_ref[kvh, e, c], preferred_element_type=F32)

    def row_max(m_part):
        return jnp.broadcast_to(jnp.max(m_part, axis=-1, keepdims=True), (tq, LANES))

    def q_pair(qt, h):
        pair = head_ids(h)[2]
        return q_ref[0, pl.ds(pl.multiple_of(qt * tq, tq), tq), pair * LANES:(pair + 1) * LANES]

    neg = jnp.full((tq, LANES), -jnp.inf, F32)
    zero = jnp.zeros((tq, LANES), F32)

    qp0 = q_pair(0, 0)
    m0 = row_max(lax.fori_loop(0, n_kv, lambda c, m: scores_chunk(qp0, 0, c, m), neg))

    def q_tile(qt, m_head0):
        m_cur = m_head0
        r_even = None
        for h in range(n_heads):
            h_next = (h + 1) % n_heads
            qt_next = qt if h_next else jnp.minimum(qt + 1, n_q - 1)
            qp = q_pair(qt_next, h_next)
            m_part, r = neg, zero
            for c in range(n_kv):
                m_part = scores_chunk(qp, h_next, c, m_part)
                r = pv_chunk(h, c, m_cur, r)
            m_cur = row_max(m_part)
            if h % 2 == 0:
                r_even = r
            else:
                pair = h // 2
                lane = lax.broadcasted_iota(jnp.int32, r.shape, 1)
                o_pair = jnp.where(lane < HEAD_DIM,
                                   r_even / pltpu.roll(r_even, HEAD_DIM, axis=1),
                                   r / pltpu.roll(r, HEAD_DIM, axis=1))
                o_ref[0, pl.ds(pl.multiple_of(qt * tq, tq), tq), pair * LANES:(pair + 1) * LANES] = (
                    o_pair.astype(BF16))
        return m_cur

    lax.fori_loop(0, n_q, q_tile, m0)


def _attention(q, k, v):
    bsz, seq, q_w = q.shape
    kv_w = k.shape[-1]
    tq, tk = ATTN_Q_TILE, ATTN_KV_TILE
    n_heads = q_w // HEAD_DIM
    kern = functools.partial(_attn_kernel, seq=seq, tq=tq, tk=tk, n_heads=n_heads)

    def row(width):
        return pl.BlockSpec((1, seq, width), lambda b: (b, 0, 0))

    return pl.pallas_call(
        kern,
        grid=(bsz,),
        in_specs=[row(q_w), row(kv_w), row(kv_w)],
        out_specs=row(q_w),
        out_shape=jax.ShapeDtypeStruct((bsz, seq, q_w), BF16),
        scratch_shapes=[
            pltpu.VMEM((N_KV_HEADS, 2, seq // tk, LANES, tk), BF16),
            pltpu.VMEM((N_KV_HEADS, 2, seq // tk, tk, LANES), BF16),
            pltpu.VMEM((2, seq // tk, tq, tk), F32),
        ],
        compiler_params=_cparams(("arbitrary",)),
        name="attention",
    )(q, k, v)


def _residual_mlp_tail(x, mix, mod, nw2_ref, w1_ref, w2_ref, hmid_ref, fw_ref, o_ref, final):
    h1 = x + mod[2:3] * mix
    u = _norm_mod(h1, nw2_ref[...], mod[4:5], mod[3:4]).astype(BF16)
    d_ff = w1_ref.shape[1]
    for f in range(d_ff // FF_TILE):
        cols = slice(f * FF_TILE, (f + 1) * FF_TILE)
        a = jnp.maximum(jnp.dot(u, w1_ref[:, cols], preferred_element_type=F32), 0.0)
        hmid_ref[:, cols] = (a * a).astype(BF16)
    ff = jnp.dot(hmid_ref[...], w2_ref[...], preferred_element_type=F32)
    h2 = h1 + mod[5:6] * ff
    if final:
        rs = lax.rsqrt(jnp.mean(h2 * h2, axis=-1, keepdims=True) + EPS)
        h2 = h2 * rs * fw_ref[...]
    o_ref[0] = h2


def _hyb_out_kernel(zs5_ref, yf_ref, yb_ref, att_ref, h_ref, mod_ref, d_ref, gluw_ref, glub_ref,
                    wout_ref, nw2_ref, w1_ref, w2_ref, fw_ref, o_ref, hmid_ref, *, s5_w, final):
    mod = mod_ref[0]
    y = _cat_lane_blocks(zs5_ref) * d_ref[...] + _cat_lane_blocks(yf_ref) + _cat_lane_blocks(yb_ref)
    g = _gelu(y)
    gate = _sigmoid(jnp.dot(g.astype(BF16), gluw_ref[...], preferred_element_type=F32) + glub_ref[...])
    ys5 = (g * gate).astype(BF16)
    mix = (jnp.dot(ys5, wout_ref[:s5_w, :], preferred_element_type=F32)
           + jnp.dot(att_ref[0], wout_ref[s5_w:, :], preferred_element_type=F32))
    _residual_mlp_tail(h_ref[0], mix, mod, nw2_ref, w1_ref, w2_ref, hmid_ref, fw_ref, o_ref, final)


def _hyb_out(zs5, yf, yb, att, h, mod, d_row, glu_w, glu_b, w_out, nw2, w1, w2, fw, final):
    bsz, seq, d = h.shape
    ts = TOKEN_TILE
    s5_w = glu_w.shape[0]
    d_ff = w1.shape[1]

    def tok(width):
        return pl.BlockSpec((1, ts, width), lambda b, s: (b, s, 0))

    kern = functools.partial(_hyb_out_kernel, s5_w=s5_w, final=final)
    return pl.pallas_call(
        kern,
        grid=(bsz, seq // ts),
        in_specs=[
            _lane_block_spec(s5_w, ts), _lane_block_spec(s5_w, ts), _lane_block_spec(s5_w, ts),
            tok(att.shape[-1]), tok(d),
            pl.BlockSpec((1, 6, d), lambda b, s: (b, 0, 0)),
            _const_spec((1, s5_w)),
            _const_spec(glu_w.shape),
            _const_spec((1, s5_w)),
            _const_spec(w_out.shape),
            _const_spec((1, d)),
            _const_spec(w1.shape),
            _const_spec(w2.shape),
            _const_spec((1, d)),
        ],
        out_specs=tok(d),
        out_shape=jax.ShapeDtypeStruct((bsz, seq, d), F32),
        scratch_shapes=[pltpu.VMEM((ts, d_ff), BF16)],
        compiler_params=_cparams(("arbitrary", "arbitrary")),
        name="hyb_out_mlp",
    )(zs5, yf, yb, att, h, mod, d_row, glu_w, glu_b, w_out, nw2, w1, w2, fw)


def _rec_in_kernel(h_ref, mod_ref, nw_ref, w_ref, gate_ref, xr_ref, *, width):
    mod = mod_ref[0]
    u = _norm_mod(h_ref[0], nw_ref[...], mod[1:2], mod[0:1]).astype(BF16)
    gate_ref[0] = jnp.dot(u, w_ref[:, :width], preferred_element_type=F32).astype(BF16)
    xr_ref[0] = jnp.dot(u, w_ref[:, width:], preferred_element_type=F32)


def _rec_in(h, mod, nw, w_in):
    bsz, seq, d = h.shape
    ts = TOKEN_TILE
    width = w_in.shape[1] // 2
    kern = functools.partial(_rec_in_kernel, width=width)

    def tok(w):
        return pl.BlockSpec((1, ts, w), lambda b, s: (b, s, 0))

    return pl.pallas_call(
        kern,
        grid=(bsz, seq // ts),
        in_specs=[
            tok(d),
            pl.BlockSpec((1, 6, d), lambda b, s: (b, 0, 0)),
            _const_spec((1, d)),
            _const_spec(w_in.shape),
        ],
        out_specs=[tok(width), tok(width)],
        out_shape=[
            jax.ShapeDtypeStruct((bsz, seq, width), BF16),
            jax.ShapeDtypeStruct((bsz, seq, width), F32),
        ],
        compiler_params=_cparams(("arbitrary", "arbitrary")),
        name="rec_in",
    )(h, mod, nw, w_in)


def _quarter_one_minus_exp(x, log2_a, a):
    series = (-0.5 * LN_2) * log2_a * (1.0 + x * (1 / 2 + x * (1 / 6)))
    return jnp.where(x > -1 / 32, series, 0.25 - 0.25 * (a * a))


def _lru_direction(main_ref, prev_ref, next_ref, at_start, at_end, convw_ref, convb_ref,
                   wr_ref, wi_ref, rab_ref, ixb_ref, nsp_ref, out_ref,
                   xext_ref, a_ref, hs_ref, carry_ref, *, tt, reverse):
    bsz = main_ref.shape[0]
    rows = tt * SUBLANES
    halo_t = prev_ref.shape[1]
    halo = halo_t * SUBLANES
    width = main_ref.shape[2]
    n_lane_blk = width // LANES
    for b in range(bsz):
        xp = jnp.where(at_start, 0.0, prev_ref[b])
        xm = main_ref[b]
        xn = jnp.where(at_end, 0.0, next_ref[b])
        for l in range(n_lane_blk):
            lanes = slice(l * LANES, (l + 1) * LANES)
            xext_ref[l, _tm_rows(b, halo_t), :] = xp[:, lanes]
            xext_ref[l, _tm_rows(b, tt, halo), :] = xm[:, lanes]
            xext_ref[l, _tm_rows(b, halo_t, halo + rows), :] = xn[:, lanes]
    blk = LRU_HEADS_PER_BLOCK * (width // LRU_HEADS)
    lane_per_blk = blk // LANES
    for j in range(width // blk):
        cols = slice(j * blk, (j + 1) * blk)
        for r0 in range(0, rows, LRU_GATE_ROWS):
            parts = []
            for l in range(j * lane_per_blk, (j + 1) * lane_per_blk):
                lanes = slice(l * LANES, (l + 1) * LANES)
                acc = convb_ref[:, lanes]
                for tap in range(CONV_WIDTH):
                    first = halo + (tap - CONV_LEFT) * SUBLANES + r0
                    acc = acc + convw_ref[tap:tap + 1, lanes] * xext_ref[l, first:first + LRU_GATE_ROWS, :]
                parts.append(acc)
            xc = jnp.concatenate(parts, axis=-1)
            xcb = xc.astype(BF16)
            t_r = jnp.tanh(jnp.dot(xcb, wr_ref[j], preferred_element_type=F32) + rab_ref[:, cols])
            t_i = jnp.tanh(jnp.dot(xcb, wi_ref[j], preferred_element_type=F32) + ixb_ref[:, cols])
            log2_a = nsp_ref[:, cols] * (t_r + 1.0)
            a = jnp.exp2(log2_a)
            om = _quarter_one_minus_exp((2.0 * LN_2) * log2_a, log2_a, a)
            bt = (om * lax.rsqrt(jnp.maximum(om, SQRT_FLOOR))) * ((t_i + 1.0) * xc)
            for k in range(lane_per_blk):
                l = j * lane_per_blk + k
                a_ref[l, r0:r0 + LRU_GATE_ROWS, :] = a[:, k * LANES:(k + 1) * LANES]
                hs_ref[l, r0:r0 + LRU_GATE_ROWS, :] = bt[:, k * LANES:(k + 1) * LANES]

    def body(k, h):
        t = (tt - 1 - k) if reverse else k
        r_t = pl.ds(pl.multiple_of(t * SUBLANES, SUBLANES), SUBLANES)
        new = []
        for l in range(n_lane_blk):
            h_l = a_ref[l, r_t, :] * h[l] + hs_ref[l, r_t, :]
            hs_ref[l, r_t, :] = h_l
            new.append(h_l)
        return tuple(new)

    h0 = tuple(carry_ref[:, l * LANES:(l + 1) * LANES] for l in range(n_lane_blk))
    h_end = lax.fori_loop(0, tt, body, h0, unroll=8)
    for l in range(n_lane_blk):
        carry_ref[:, l * LANES:(l + 1) * LANES] = h_end[l]
    for b in range(bsz):
        out_ref[b] = jnp.concatenate(
            [hs_ref[l, _tm_rows(b, tt), :] for l in range(n_lane_blk)], axis=-1).astype(BF16)


def _lru_kernel(fm_ref, fp_ref, fn_ref, bm_ref, bp_ref, bn_ref, convw_ref, convb_ref,
                wr_ref, wi_ref, rab_ref, ixb_ref, nsp_ref, hf_ref, hb_ref,
                xext_ref, a_ref, hs_ref, carry_ref, *, tt):
    i = pl.program_id(0)
    last = pl.num_programs(0) - 1

    @pl.when(i == 0)
    def _():
        carry_ref[...] = jnp.zeros_like(carry_ref)

    _lru_direction(fm_ref, fp_ref, fn_ref, i == 0, i == last, convw_ref, convb_ref,
                   wr_ref.at[0], wi_ref.at[0], rab_ref.at[0], ixb_ref.at[0], nsp_ref.at[0], hf_ref,
                   xext_ref, a_ref, hs_ref, carry_ref.at[0], tt=tt, reverse=False)
    _lru_direction(bm_ref, bp_ref, bn_ref, i == last, i == 0, convw_ref, convb_ref,
                   wr_ref.at[1], wi_ref.at[1], rab_ref.at[1], ixb_ref.at[1], nsp_ref.at[1], hb_ref,
                   xext_ref, a_ref, hs_ref, carry_ref.at[1], tt=tt, reverse=True)


def _lru_scan(xr, conv_w, conv_b, wr, wi, ra_b, ix_b, nsp):
    bsz, seq, width = xr.shape
    tt = LRU_TIME_TILE
    rows = tt * bsz
    nc = seq // tt
    halo_t = SUBLANES
    per = tt // halo_t
    n_halo = seq // halo_t

    def main_f(i):
        return (0, i, 0)

    def prev_f(i):
        return (0, jnp.maximum(i * per - 1, 0), 0)

    def next_f(i):
        return (0, jnp.minimum((i + 1) * per, n_halo - 1), 0)

    def rev(f):
        return lambda i: f(nc - 1 - i)

    main_blk = (bsz, tt, width)
    halo_blk = (bsz, halo_t, width)
    kern = functools.partial(_lru_kernel, tt=tt)
    return pl.pallas_call(
        kern,
        grid=(nc,),
        in_specs=[
            pl.BlockSpec(main_blk, main_f),
            pl.BlockSpec(halo_blk, prev_f),
            pl.BlockSpec(halo_blk, next_f),
            pl.BlockSpec(main_blk, rev(main_f)),
            pl.BlockSpec(halo_blk, rev(prev_f)),
            pl.BlockSpec(halo_blk, rev(next_f)),
            _const_spec(conv_w.shape),
            _const_spec(conv_b.shape),
            _const_spec(wr.shape),
            _const_spec(wi.shape),
            _const_spec(ra_b.shape),
            _const_spec(ix_b.shape),
            _const_spec(nsp.shape),
        ],
        out_specs=[
            pl.BlockSpec(main_blk, main_f),
            pl.BlockSpec(main_blk, rev(main_f)),
        ],
        out_shape=[jax.ShapeDtypeStruct((bsz, seq, width), BF16)] * 2,
        scratch_shapes=[
            pltpu.VMEM((width // LANES, rows + 2 * halo_t * SUBLANES, LANES), F32),
            pltpu.VMEM((width // LANES, rows, LANES), F32),
            pltpu.VMEM((width // LANES, rows, LANES), F32),
            pltpu.VMEM((2, SUBLANES, width), F32),
        ],
        compiler_params=_cparams(("arbitrary",)),
        name="lru_scan",
    )(xr, xr, xr, xr, xr, xr, conv_w, conv_b, wr, wi, ra_b, ix_b, nsp)


def _rec_out_kernel(hf_ref, hb_ref, gate_ref, h_ref, mod_ref, wout_ref, nw2_ref, w1_ref, w2_ref,
                    fw_ref, o_ref, hmid_ref, *, final):
    mod = mod_ref[0]
    y = (hf_ref[0].astype(F32) + hb_ref[0].astype(F32)) * _gelu(gate_ref[0].astype(F32))
    mix = jnp.dot(y.astype(BF16), wout_ref[...], preferred_element_type=F32)
    _residual_mlp_tail(h_ref[0], mix, mod, nw2_ref, w1_ref, w2_ref, hmid_ref, fw_ref, o_ref, final)


def _rec_out(hf, hb, gate, h, mod, w_out, nw2, w1, w2, fw, final):
    bsz, seq, d = h.shape
    ts = TOKEN_TILE
    width = w_out.shape[0]
    d_ff = w1.shape[1]

    def tok(w):
        return pl.BlockSpec((1, ts, w), lambda b, s: (b, s, 0))

    kern = functools.partial(_rec_out_kernel, final=final)
    return pl.pallas_call(
        kern,
        grid=(bsz, seq // ts),
        in_specs=[
            tok(width), tok(width), tok(width), tok(d),
            pl.BlockSpec((1, 6, d), lambda b, s: (b, 0, 0)),
            _const_spec(w_out.shape),
            _const_spec((1, d)),
            _const_spec(w1.shape),
            _const_spec(w2.shape),
            _const_spec((1, d)),
        ],
        out_specs=tok(d),
        out_shape=jax.ShapeDtypeStruct((bsz, seq, d), F32),
        scratch_shapes=[pltpu.VMEM((ts, d_ff), BF16)],
        compiler_params=_cparams(("arbitrary", "arbitrary")),
        name="rec_out_mlp",
    )(hf, hb, gate, h, mod, w_out, nw2, w1, w2, fw)


def _s5_discretize(lam_re, lam_im, log_dt, b_re, b_im, c_re, c_im):
    lr = jnp.minimum(lam_re, S5_MAX_RE)
    li = lam_im
    dt = jnp.exp(log_dt)[..., None]
    mag = jnp.exp(lr * dt)
    abar_re = mag * jnp.cos(li * dt)
    abar_im = mag * jnp.sin(li * dt)
    den = lr * lr + li * li
    nr = abar_re - 1.0
    f_re = (nr * lr + abar_im * li) / den
    f_im = (abar_im * lr - nr * li) / den
    bb_re = f_re[..., None] * b_re - f_im[..., None] * b_im
    bb_im = f_re[..., None] * b_im + f_im[..., None] * b_re
    n_dir, groups, p, c = bb_re.shape
    gpb = S5_GROUPS_PER_BLOCK
    n_blk = groups // gpb
    eye = jnp.eye(gpb, dtype=F32)

    def pack_b(bb):
        bb = bb.reshape(n_dir, n_blk, gpb, p, c)
        return jnp.einsum("djgpc,gh->djgchp", bb, eye).reshape(n_dir, n_blk, gpb * c, gpb * p)

    def pack_c(cc):
        cc = cc.reshape(n_dir, n_blk, gpb, c, p)
        return jnp.einsum("djgcp,gh->djgphc", cc, eye).reshape(n_dir, n_blk, gpb * p, gpb * c)

    def pack_d(dd):
        dd = dd.reshape(n_dir, n_blk, gpb, c, c)
        return jnp.einsum("djgio,gh->djgiho", dd, eye).reshape(n_dir, n_blk, gpb * c, gpb * c)

    ar, ai = abar_re[..., None], abar_im[..., None]
    abb_re = ar * bb_re - ai * bb_im
    abb_im = ar * bb_im + ai * bb_re
    b_plain = jnp.concatenate([pack_b(bb_re), pack_b(bb_im)], axis=-1)
    b_adv = jnp.concatenate([pack_b(abb_re), pack_b(abb_im)], axis=-1)
    bpair = jnp.stack([jnp.concatenate([b_adv[0], b_plain[0]], axis=-2),
                       jnp.concatenate([b_plain[1], b_adv[1]], axis=-2)]).astype(BF16)

    cr, ci = abar_re[:, :, None, :], abar_im[:, :, None, :]
    ca_re = c_re * cr - c_im * ci
    ca_im = c_re * ci + c_im * cr
    c_plain = jnp.concatenate([pack_c(c_re), -pack_c(c_im)], axis=-2)
    c_adv = jnp.concatenate([pack_c(ca_re), -pack_c(ca_im)], axis=-2)
    cpair = jnp.stack([jnp.concatenate([c_plain[0], c_adv[0]], axis=-1),
                       jnp.concatenate([c_adv[1], c_plain[1]], axis=-1)]).astype(BF16)

    direct = pack_d(jnp.einsum("dgop,dgpi->dgio", c_re, bb_re) - jnp.einsum("dgop,dgpi->dgio", c_im, bb_im))
    zeros = jnp.zeros_like(direct[0])
    dmat = jnp.stack([jnp.concatenate([zeros, direct[0]], axis=-1),
                      jnp.concatenate([direct[1], zeros], axis=-1)]).astype(BF16)

    a2_re = (abar_re * abar_re - abar_im * abar_im).reshape(n_dir, n_blk, gpb * p)
    a2_im = (2.0 * abar_re * abar_im).reshape(n_dir, n_blk, gpb * p)
    return bpair, a2_re, a2_im, cpair, dmat


def _rope_tables(seq):
    rows = seq // GRID_W
    row_idx = jnp.repeat(jnp.arange(rows, dtype=F32), GRID_W)
    col_idx = jnp.tile(jnp.arange(GRID_W, dtype=F32), rows)
    inv_freq = ROPE_BASE ** (-jnp.arange(ROPE_FREQS, dtype=F32) / ROPE_FREQS)
    ang_r = row_idx[:, None] * inv_freq
    ang_c = col_idx[:, None] * inv_freq
    cos = jnp.concatenate([jnp.cos(ang_r)] * 2 + [jnp.cos(ang_c)] * 2, axis=-1)
    sin = jnp.concatenate([-jnp.sin(ang_r), jnp.sin(ang_r), -jnp.sin(ang_c), jnp.sin(ang_c)], axis=-1)
    reps = LANES // HEAD_DIM
    return jnp.tile(cos, (1, reps)), jnp.tile(sin, (1, reps))


def _block_diag_heads(w):
    n_dir, heads, i, j = w.shape
    hpb = LRU_HEADS_PER_BLOCK
    eye = jnp.eye(hpb, dtype=F32)
    w = w.reshape(n_dir, heads // hpb, hpb, i, j)
    return jnp.einsum("dnhij,hg->dnhigj", w, eye).reshape(n_dir, heads // hpb, hpb * i, hpb * j).astype(BF16)


def kernel(x, c, norm_w, ada_w, ada_b, mlp_w1, mlp_w2, final_norm_w, hyb_w_in, s5_lam_re, s5_lam_im,
           s5_log_dt, s5_b_re, s5_b_im, s5_c_re, s5_c_im, s5_d, s5_glu_w, s5_glu_b, attn_q_norm,
           attn_k_norm, hyb_w_out, rec_w_in, rec_conv_w, rec_conv_b, rec_ra_w, rec_ra_b, rec_ix_w,
           rec_ix_b, rec_lam, rec_w_out):
    bsz, seq, d = x.shape
    depth = norm_w.shape[0]
    assert bsz == SUBLANES, "the time-major recurrent layout needs one sublane tile of batch rows"
    s5_w = s5_d.shape[-1]
    kv_w = N_KV_HEADS * HEAD_DIM
    q_w = hyb_w_in.shape[-1] - s5_w - 2 * kv_w

    mod_all = _ada_mod(c, ada_w, ada_b).reshape(depth, bsz, 6, d)
    cos, sin = _rope_tables(seq)
    reps = LANES // HEAD_DIM
    head_id = jnp.arange(LANES) // HEAD_DIM
    g_ones = (head_id[:, None] == head_id[None, :]).astype(BF16)
    fw = final_norm_w.reshape(1, d)

    h = x
    for layer in range(depth):
        mod = mod_all[layer]
        nw1 = norm_w[layer, 0].reshape(1, d)
        nw2 = norm_w[layer, 1].reshape(1, d)
        w1 = mlp_w1[layer].astype(BF16)
        w2 = mlp_w2[layer].astype(BF16)
        final = layer == depth - 1
        if layer % 2 == 0:
            e = layer // 2
            qn_row = jnp.tile(attn_q_norm[e] * (HEAD_DIM ** -0.5 * LOG2_E), reps).reshape(1, LANES)
            kn_row = jnp.tile(attn_k_norm[e], reps).reshape(1, LANES)
            zs5, q, k, v = _hyb_in(h, mod, nw1, hyb_w_in[e].astype(BF16), g_ones, qn_row, kn_row,
                                   cos, sin, s5_w, q_w, kv_w)
            bpair, a2_re, a2_im, cpair, dmat = _s5_discretize(s5_lam_re[e], s5_lam_im[e], s5_log_dt[e],
                                                              s5_b_re[e], s5_b_im[e], s5_c_re[e], s5_c_im[e])
            yf, yb = _s5_scan(zs5, bpair, a2_re, a2_im, cpair, dmat)
            att = _attention(q, k, v)
            h = _hyb_out(zs5, yf, yb, att, h, mod, s5_d[e].reshape(1, s5_w), s5_glu_w[e].astype(BF16),
                         s5_glu_b[e].reshape(1, s5_w), hyb_w_out[e].astype(BF16), nw2, w1, w2, fw, final)
        else:
            o = layer // 2
            width = rec_w_out.shape[1]
            gate, xr = _rec_in(h, mod, nw1, rec_w_in[o].astype(BF16))
            nsp = ((-0.5 * LRU_C * LOG2_E) * jax.nn.softplus(-rec_lam[o])).reshape(2, 1, width)
            hf, hb = _lru_scan(xr, rec_conv_w[o], rec_conv_b[o].reshape(1, width),
                               _block_diag_heads(0.5 * rec_ra_w[o]), _block_diag_heads(0.5 * rec_ix_w[o]),
                               (0.5 * rec_ra_b[o]).reshape(2, 1, width), (0.5 * rec_ix_b[o]).reshape(2, 1, width),
                               nsp)
            h = _rec_out(hf, hb, gate, h, mod, rec_w_out[o].astype(BF16), nw2, w1, w2, fw, final)
    return h
```

```python
import functools
import math

import jax
import jax.numpy as jnp
from jax import lax
from jax.experimental import pallas as pl
from jax.experimental.pallas import tpu as pltpu

F32 = jnp.float32
BF16 = jnp.bfloat16

EPS = 1e-6
GRID_W = 64

S5_GROUP = 16
S5_STATE = 64
S5_MAX_RE = -1e-4
S5_GROUPS_PER_BLOCK = 8
S5_BLOCK_IN = S5_GROUPS_PER_BLOCK * S5_GROUP
S5_BLOCK_STATE = S5_GROUPS_PER_BLOCK * S5_STATE

HEAD_DIM = 64
N_KV_HEADS = 2
ROPE_FREQS = HEAD_DIM // 4
ROPE_BASE = 10000.0

LRU_HEADS = 16
LRU_HEADS_PER_BLOCK = 4
CONV_WIDTH = 4
CONV_LEFT = 2
LRU_C = 8.0

SUBLANES = 8
LANES = 128

TOKEN_TILE = 512
ATTN_Q_TILE = 512
ATTN_KV_TILE = 512
S5_TIME_TILE = 64
LRU_TIME_TILE = 32
LRU_GATE_ROWS = 128
FF_TILE = 512
VMEM_LIMIT = 56 * 1024 * 1024

SQRT_FLOOR = 1e-37
LOG2_E = math.log2(math.e)
LN_2 = math.log(2.0)


def _cparams(sem):
    return pltpu.CompilerParams(dimension_semantics=sem, vmem_limit_bytes=VMEM_LIMIT)


def _const_spec(shape):
    nd = len(shape)
    return pl.BlockSpec(shape, lambda *_: (0,) * nd, pipeline_mode=pl.Buffered(1))


def _norm_mod(x, w_row, sc_row, sh_row):
    rs = lax.rsqrt(jnp.mean(x * x, axis=-1, keepdims=True) + EPS)
    return x * rs * (w_row * (1.0 + sc_row)) + sh_row


def _sigmoid(x):
    return 1.0 / (1.0 + jnp.exp(-x))


def _gelu(x):
    return jax.nn.gelu(x, approximate=True)


def _lane_block_spec(width, ts):
    return pl.BlockSpec((1, width // LANES, ts, LANES), lambda b, s: (b, 0, s, 0))


def _cat_lane_blocks(ref):
    return jnp.concatenate([ref[0, j] for j in range(ref.shape[1])], axis=-1)


def _tm_rows(b, tt, first_row=0):
    return pl.ds(first_row + b, tt, stride=SUBLANES)


def _ada_kernel(c_ref, w_ref, b_ref, o_ref):
    c = c_ref[...]
    ca = (c * _sigmoid(c)).astype(BF16)
    o_ref[0] = jnp.dot(ca, w_ref[0].astype(BF16), preferred_element_type=F32) + b_ref[0]


def _ada_mod(c, ada_w, ada_b):
    depth, d, n = ada_w.shape
    bsz = c.shape[0]
    tn = 1024
    return pl.pallas_call(
        _ada_kernel,
        grid=(depth, n // tn),
        in_specs=[
            pl.BlockSpec((bsz, d), lambda l, j: (0, 0)),
            pl.BlockSpec((1, d, tn), lambda l, j: (l, 0, j)),
            pl.BlockSpec((1, 1, tn), lambda l, j: (l, 0, j)),
        ],
        out_specs=pl.BlockSpec((1, bsz, tn), lambda l, j: (l, 0, j)),
        out_shape=jax.ShapeDtypeStruct((depth, bsz, n), F32),
        compiler_params=_cparams(("arbitrary", "arbitrary")),
        name="ada_mod",
    )(c, ada_w, ada_b.reshape(depth, 1, n))


def _head_norm_rope(z, g_ones, nw_row, cos, sin_signed):
    ss = jnp.dot((z * z).astype(BF16), g_ones, preferred_element_type=F32)
    zn = z * lax.rsqrt(ss * (1.0 / HEAD_DIM) + EPS) * nw_row
    lane = lax.broadcasted_iota(jnp.int32, zn.shape, 1)
    first_half = (lane % (2 * ROPE_FREQS)) < ROPE_FREQS
    partner = jnp.where(first_half,
                        pltpu.roll(zn, LANES - ROPE_FREQS, axis=1),
                        pltpu.roll(zn, ROPE_FREQS, axis=1))
    return zn * cos + partner * sin_signed


def _hyb_in_kernel(h_ref, mod_ref, nw_ref, w_ref, gones_ref, qn_ref, kn_ref, cos_ref, sin_ref,
                   zs5_ref, q_ref, k_ref, v_ref, *, s5_w, q_w, kv_w):
    x = h_ref[0]
    mod = mod_ref[0]
    u = _norm_mod(x, nw_ref[...], mod[1:2], mod[0:1]).astype(BF16)
    z = jnp.dot(u, w_ref[...], preferred_element_type=F32)
    for j in range(s5_w // LANES):
        zs5_ref[0, j] = z[:, j * LANES:(j + 1) * LANES]
    cos = cos_ref[...]
    sin = sin_ref[...]
    g_ones = gones_ref[...]
    for j in range(q_w // LANES):
        zq = z[:, s5_w + j * LANES: s5_w + (j + 1) * LANES]
        q_ref[0, :, j * LANES:(j + 1) * LANES] = _head_norm_rope(
            zq, g_ones, qn_ref[...], cos, sin).astype(BF16)
    for j in range(kv_w // LANES):
        zk = z[:, s5_w + q_w + j * LANES: s5_w + q_w + (j + 1) * LANES]
        k_ref[0, :, j * LANES:(j + 1) * LANES] = _head_norm_rope(
            zk, g_ones, kn_ref[...], cos, sin).astype(BF16)
    v_ref[0] = z[:, s5_w + q_w + kv_w:].astype(BF16)


def _hyb_in(h, mod, nw, w_in, g_ones, qn_row, kn_row, cos, sin, s5_w, q_w, kv_w):
    bsz, seq, d = h.shape
    ts = TOKEN_TILE
    n = w_in.shape[1]
    kern = functools.partial(_hyb_in_kernel, s5_w=s5_w, q_w=q_w, kv_w=kv_w)

    def tok(width):
        return pl.BlockSpec((1, ts, width), lambda b, s: (b, s, 0))

    return pl.pallas_call(
        kern,
        grid=(bsz, seq // ts),
        in_specs=[
            tok(d),
            pl.BlockSpec((1, 6, d), lambda b, s: (b, 0, 0)),
            _const_spec((1, d)),
            _const_spec((d, n)),
            _const_spec((LANES, LANES)),
            _const_spec((1, LANES)),
            _const_spec((1, LANES)),
            pl.BlockSpec((ts, LANES), lambda b, s: (s, 0)),
            pl.BlockSpec((ts, LANES), lambda b, s: (s, 0)),
        ],
        out_specs=[_lane_block_spec(s5_w, ts), tok(q_w), tok(kv_w), tok(kv_w)],
        out_shape=[
            jax.ShapeDtypeStruct((bsz, s5_w // LANES, seq, LANES), F32),
            jax.ShapeDtypeStruct((bsz, seq, q_w), BF16),
            jax.ShapeDtypeStruct((bsz, seq, kv_w), BF16),
            jax.ShapeDtypeStruct((bsz, seq, kv_w), BF16),
        ],
        compiler_params=_cparams(("arbitrary", "arbitrary")),
        name="hyb_in",
    )(h, mod, nw, w_in, g_ones, qn_row, kn_row, cos, sin)


def _s5_kernel(u0_ref, u1_ref, bpair_ref, are_ref, aim_ref, cpair_ref, dmat_ref, y0_ref, y1_ref,
               tm_ref, ys_ref, bu_ref, carry_ref, *, tt, n_blk):
    bsz = u0_ref.shape[0]
    w = 2 * S5_BLOCK_STATE
    pairs = tt // 2
    prow = pairs * SUBLANES
    even, odd = 0, 1
    base = {(0, even): 0, (0, odd): SUBLANES, (1, even): 0, (1, odd): 0}

    @pl.when(pl.program_id(0) == 0)
    def _():
        carry_ref[...] = jnp.zeros_like(carry_ref)
        tm_ref[0, odd, :, 0:SUBLANES, :] = jnp.zeros((n_blk, SUBLANES, LANES), F32)
        tm_ref[1, even, :, prow:prow + SUBLANES, :] = jnp.zeros((n_blk, SUBLANES, LANES), F32)

    def to_time_major(d, u_ref):
        for b in range(bsz):
            for j in range(n_blk):
                for parity in (even, odd):
                    tm_ref[d, parity, j, _tm_rows(b, pairs, base[d, parity]), :] = (
                        u_ref[b, j, pl.ds(parity, pairs, stride=2), :])

    def pair_lhs(d, j):
        if d == 0:
            first, second = tm_ref[0, odd, j, 0:prow, :], tm_ref[0, even, j, 0:prow, :]
        else:
            first, second = tm_ref[1, odd, j, 0:prow, :], tm_ref[1, even, j, SUBLANES:SUBLANES + prow, :]
        return jnp.concatenate([first, second], axis=-1).astype(BF16)

    def project_in(d, j):
        bu_ref[d, :, j * w:(j + 1) * w] = jnp.dot(pair_lhs(d, j), bpair_ref[d, j], preferred_element_type=F32)

    def keep_neighbour_tile(d):
        for j in range(n_blk):
            if d == 0:
                tm_ref[0, odd, j, 0:SUBLANES, :] = tm_ref[0, odd, j, prow:prow + SUBLANES, :]
            else:
                tm_ref[1, even, j, prow:prow + SUBLANES, :] = tm_ref[1, even, j, 0:SUBLANES, :]

    def recurrence(d, j):
        re_cols = slice(j * w, j * w + S5_BLOCK_STATE)
        im_cols = slice(j * w + S5_BLOCK_STATE, (j + 1) * w)
        a_re = jnp.broadcast_to(are_ref[d, j:j + 1, :], (SUBLANES, S5_BLOCK_STATE))
        a_im = jnp.broadcast_to(aim_ref[d, j:j + 1, :], (SUBLANES, S5_BLOCK_STATE))
        h_re = carry_ref[d, :, re_cols]
        h_im = carry_ref[d, :, im_cols]
        for k in range(pairs):
            m = (pairs - 1 - k) if d else k
            rows = slice(m * SUBLANES, (m + 1) * SUBLANES)
            h_re, h_im = (a_re * h_re - a_im * h_im + bu_ref[d, rows, re_cols],
                          a_re * h_im + a_im * h_re + bu_ref[d, rows, im_cols])
            bu_ref[d, rows, re_cols] = h_re
            bu_ref[d, rows, im_cols] = h_im
        carry_ref[d, :, re_cols] = h_re
        carry_ref[d, :, im_cols] = h_im

    def project_out(d, j):
        direct_parity = odd if d == 0 else even
        first = base[d, direct_parity]
        u_direct = tm_ref[d, direct_parity, j, first:first + prow, :].astype(BF16)
        y = (jnp.dot(bu_ref[d, :, j * w:(j + 1) * w].astype(BF16), cpair_ref[d, j], preferred_element_type=F32)
             + jnp.dot(u_direct, dmat_ref[d, j], preferred_element_type=F32))
        ys_ref[d, even, j] = y[:, :LANES]
        ys_ref[d, odd, j] = y[:, LANES:]

    def to_batch_major(d, y_ref):
        for b in range(bsz):
            for j in range(n_blk):
                for parity in (even, odd):
                    y_ref[b, j, pl.ds(parity, pairs, stride=2), :] = ys_ref[d, parity, j, _tm_rows(b, pairs), :]

    to_time_major(0, u0_ref)
    to_time_major(1, u1_ref)
    for j in range(n_blk):
        project_in(0, j)
    for j in range(n_blk):
        recurrence(0, j)
        project_in(1, j)
    for j in range(n_blk):
        project_out(0, j)
        recurrence(1, j)
    keep_neighbour_tile(0)
    to_batch_major(0, y0_ref)
    for j in range(n_blk):
        project_out(1, j)
    keep_neighbour_tile(1)
    to_batch_major(1, y1_ref)


def _s5_scan(u, bpair, a_re, a_im, cpair, dmat):
    bsz, n_blk, seq, _ = u.shape
    tt = S5_TIME_TILE
    prow = (tt // 2) * bsz
    nc = seq // tt
    state_w = n_blk * 2 * S5_BLOCK_STATE
    kern = functools.partial(_s5_kernel, tt=tt, n_blk=n_blk)
    fwd = pl.BlockSpec((bsz, n_blk, tt, LANES), lambda i: (0, 0, i, 0))
    bwd = pl.BlockSpec((bsz, n_blk, tt, LANES), lambda i: (0, 0, nc - 1 - i, 0))
    return pl.pallas_call(
        kern,
        grid=(nc,),
        in_specs=[fwd, bwd, _const_spec(bpair.shape), _const_spec(a_re.shape), _const_spec(a_im.shape),
                  _const_spec(cpair.shape), _const_spec(dmat.shape)],
        out_specs=[fwd, bwd],
        out_shape=[jax.ShapeDtypeStruct(u.shape, F32)] * 2,
        scratch_shapes=[
            pltpu.VMEM((2, 2, n_blk, prow + SUBLANES, LANES), F32),
            pltpu.VMEM((2, 2, n_blk, prow, LANES), F32),
            pltpu.VMEM((2, prow, state_w), F32),
            pltpu.VMEM((2, SUBLANES, state_w), F32),
        ],
        compiler_params=_cparams(("arbitrary",)),
        name="s5_scan",
    )(u, u, bpair, a_re, a_im, cpair, dmat)


def _attn_kernel(q_ref, k_ref, v_ref, o_ref, kt_ref, vext_ref, s_ref, *, seq, tq, tk, n_heads):
    n_q = seq // tq
    n_kv = seq // tk
    heads_per_kv = n_heads // N_KV_HEADS

    def prep(c, carry):
        rows = pl.ds(pl.multiple_of(c * tk, tk), tk)
        kf = k_ref[0, rows, :].astype(F32)
        vf = v_ref[0, rows, :].astype(F32)
        lane = lax.broadcasted_iota(jnp.int32, kf.shape, 1)
        low = lane < HEAD_DIM
        k_sw = pltpu.roll(kf, HEAD_DIM, axis=1)
        v_sw = pltpu.roll(vf, HEAD_DIM, axis=1)
        kt_ref[0, 0, c] = jnp.where(low, kf, 0.0).T.astype(BF16)
        kt_ref[0, 1, c] = jnp.where(low, 0.0, k_sw).T.astype(BF16)
        kt_ref[1, 0, c] = jnp.where(low, k_sw, 0.0).T.astype(BF16)
        kt_ref[1, 1, c] = jnp.where(low, 0.0, kf).T.astype(BF16)
        vext_ref[0, 0, c] = jnp.where(low, vf, 1.0).astype(BF16)
        vext_ref[0, 1, c] = jnp.where(low, 1.0, v_sw).astype(BF16)
        vext_ref[1, 0, c] = jnp.where(low, v_sw, 1.0).astype(BF16)
        vext_ref[1, 1, c] = jnp.where(low, 1.0, vf).astype(BF16)
        return carry

    lax.fori_loop(0, n_kv, prep, 0)

    def head_ids(h):
        return h // heads_per_kv, h % 2, h // 2

    def scores_chunk(qp, h, c, m_part):
        kvh, e, _ = head_ids(h)
        s = jnp.dot(qp, kt_ref[kvh, e, c], preferred_element_type=F32)
        s_ref[h % 2, c] = s
        for j in range(tk // LANES):
            m_part = jnp.maximum(m_part, s[:, j * LANES:(j + 1) * LANES])
        return m_part

    def pv_chunk(h, c, m_full, acc):
        kvh, e, _ = head_ids(h)
        s = s_ref[h % 2, c]
        p = jnp.concatenate([jnp.exp2(s[:, j * LANES:(j + 1) * LANES] - m_full)
                             for j in range(tk // LANES)], axis=-1).astype(BF16)
        return acc + jnp.dot(p, vext_ref[kvh, e, c], preferred_element_type=F32)

    def row_max(m_part):
        return jnp.broadcast_to(jnp.max(m_part, axis=-1, keepdims=True), (tq, LANES))

    def q_pair(qt, h):
        pair = head_ids(h)[2]
        return q_ref[0, pl.ds(pl.multiple_of(qt * tq, tq), tq), pair * LANES:(pair + 1) * LANES]

    neg = jnp.full((tq, LANES), -jnp.inf, F32)
    zero = jnp.zeros((tq, LANES), F32)

    qp0 = q_pair(0, 0)
    m0 = row_max(lax.fori_loop(0, n_kv, lambda c, m: scores_chunk(qp0, 0, c, m), neg))

    def q_tile(qt, m_head0):
        m_cur = m_head0
        r_even = None
        for h in range(n_heads):
            h_next = (h + 1) % n_heads
            qt_next = qt if h_next else jnp.minimum(qt + 1, n_q - 1)
            qp = q_pair(qt_next, h_next)
            m_part, r = neg, zero
            for c in range(n_kv):
                m_part = scores_chunk(qp, h_next, c, m_part)
                r = pv_chunk(h, c, m_cur, r)
            m_cur = row_max(m_part)
            if h % 2 == 0:
                r_even = r
            else:
                pair = h // 2
                lane = lax.broadcasted_iota(jnp.int32, r.shape, 1)
                o_pair = jnp.where(lane < HEAD_DIM,
                                   r_even / pltpu.roll(r_even, HEAD_DIM, axis=1),
                                   r / pltpu.roll(r, HEAD_DIM, axis=1))
                o_ref[0, pl.ds(pl.multiple_of(qt * tq, tq), tq), pair * LANES:(pair + 1) * LANES] = (
                    o_pair.astype(BF16))
        return m_cur

    lax.fori_loop(0, n_q, q_tile, m0)


def _attention(q, k, v):
    bsz, seq, q_w = q.shape
    kv_w = k.shape[-1]
    tq, tk = ATTN_Q_TILE, ATTN_KV_TILE
    n_heads = q_w // HEAD_DIM
    kern = functools.partial(_attn_kernel, seq=seq, tq=tq, tk=tk, n_heads=n_heads)

    def row(width):
        return pl.BlockSpec((1, seq, width), lambda b: (b, 0, 0))

    return pl.pallas_call(
        kern,
        grid=(bsz,),
        in_specs=[row(q_w), row(kv_w), row(kv_w)],
        out_specs=row(q_w),
        out_shape=jax.ShapeDtypeStruct((bsz, seq, q_w), BF16),
        scratch_shapes=[
            pltpu.VMEM((N_KV_HEADS, 2, seq // tk, LANES, tk), BF16),
            pltpu.VMEM((N_KV_HEADS, 2, seq // tk, tk, LANES), BF16),
            pltpu.VMEM((2, seq // tk, tq, tk), F32),
        ],
        compiler_params=_cparams(("arbitrary",)),
        name="attention",
    )(q, k, v)


def _residual_mlp_tail(x, mix, mod, nw2_ref, w1_ref, w2_ref, hmid_ref, fw_ref, o_ref, final):
    h1 = x + mod[2:3] * mix
    u = _norm_mod(h1, nw2_ref[...], mod[4:5], mod[3:4]).astype(BF16)
    d_ff = w1_ref.shape[1]
    for f in range(d_ff // FF_TILE):
        cols = slice(f * FF_TILE, (f + 1) * FF_TILE)
        a = jnp.maximum(jnp.dot(u, w1_ref[:, cols], preferred_element_type=F32), 0.0)
        hmid_ref[:, cols] = (a * a).astype(BF16)
    ff = jnp.dot(hmid_ref[...], w2_ref[...], preferred_element_type=F32)
    h2 = h1 + mod[5:6] * ff
    if final:
        rs = lax.rsqrt(jnp.mean(h2 * h2, axis=-1, keepdims=True) + EPS)
        h2 = h2 * rs * fw_ref[...]
    o_ref[0] = h2


def _hyb_out_kernel(zs5_ref, yf_ref, yb_ref, att_ref, h_ref, mod_ref, d_ref, gluw_ref, glub_ref,
                    wout_ref, nw2_ref, w1_ref, w2_ref, fw_ref, o_ref, hmid_ref, *, s5_w, final):
    mod = mod_ref[0]
    y = _cat_lane_blocks(zs5_ref) * d_ref[...] + _cat_lane_blocks(yf_ref) + _cat_lane_blocks(yb_ref)
    g = _gelu(y)
    gate = _sigmoid(jnp.dot(g.astype(BF16), gluw_ref[...], preferred_element_type=F32) + glub_ref[...])
    ys5 = (g * gate).astype(BF16)
    mix = (jnp.dot(ys5, wout_ref[:s5_w, :], preferred_element_type=F32)
           + jnp.dot(att_ref[0], wout_ref[s5_w:, :], preferred_element_type=F32))
    _residual_mlp_tail(h_ref[0], mix, mod, nw2_ref, w1_ref, w2_ref, hmid_ref, fw_ref, o_ref, final)


def _hyb_out(zs5, yf, yb, att, h, mod, d_row, glu_w, glu_b, w_out, nw2, w1, w2, fw, final):
    bsz, seq, d = h.shape
    ts = TOKEN_TILE
    s5_w = glu_w.shape[0]
    d_ff = w1.shape[1]

    def tok(width):
        return pl.BlockSpec((1, ts, width), lambda b, s: (b, s, 0))

    kern = functools.partial(_hyb_out_kernel, s5_w=s5_w, final=final)
    return pl.pallas_call(
        kern,
        grid=(bsz, seq // ts),
        in_specs=[
            _lane_block_spec(s5_w, ts), _lane_block_spec(s5_w, ts), _lane_block_spec(s5_w, ts),
            tok(att.shape[-1]), tok(d),
            pl.BlockSpec((1, 6, d), lambda b, s: (b, 0, 0)),
            _const_spec((1, s5_w)),
            _const_spec(glu_w.shape),
            _const_spec((1, s5_w)),
            _const_spec(w_out.shape),
            _const_spec((1, d)),
            _const_spec(w1.shape),
            _const_spec(w2.shape),
            _const_spec((1, d)),
        ],
        out_specs=tok(d),
        out_shape=jax.ShapeDtypeStruct((bsz, seq, d), F32),
        scratch_shapes=[pltpu.VMEM((ts, d_ff), BF16)],
        compiler_params=_cparams(("arbitrary", "arbitrary")),
        name="hyb_out_mlp",
    )(zs5, yf, yb, att, h, mod, d_row, glu_w, glu_b, w_out, nw2, w1, w2, fw)


def _rec_in_kernel(h_ref, mod_ref, nw_ref, w_ref, gate_ref, xr_ref, *, width):
    mod = mod_ref[0]
    u = _norm_mod(h_ref[0], nw_ref[...], mod[1:2], mod[0:1]).astype(BF16)
    gate_ref[0] = jnp.dot(u, w_ref[:, :width], preferred_element_type=F32).astype(BF16)
    xr_ref[0] = jnp.dot(u, w_ref[:, width:], preferred_element_type=F32)


def _rec_in(h, mod, nw, w_in):
    bsz, seq, d = h.shape
    ts = TOKEN_TILE
    width = w_in.shape[1] // 2
    kern = functools.partial(_rec_in_kernel, width=width)

    def tok(w):
        return pl.BlockSpec((1, ts, w), lambda b, s: (b, s, 0))

    return pl.pallas_call(
        kern,
        grid=(bsz, seq // ts),
        in_specs=[
            tok(d),
            pl.BlockSpec((1, 6, d), lambda b, s: (b, 0, 0)),
            _const_spec((1, d)),
            _const_spec(w_in.shape),
        ],
        out_specs=[tok(width), tok(width)],
        out_shape=[
            jax.ShapeDtypeStruct((bsz, seq, width), BF16),
            jax.ShapeDtypeStruct((bsz, seq, width), F32),
        ],
        compiler_params=_cparams(("arbitrary", "arbitrary")),
        name="rec_in",
    )(h, mod, nw, w_in)


def _quarter_one_minus_exp(x, log2_a, a):
    series = (-0.5 * LN_2) * log2_a * (1.0 + x * (1 / 2 + x * (1 / 6)))
    return jnp.where(x > -1 / 32, series, 0.25 - 0.25 * (a * a))


def _lru_direction(main_ref, prev_ref, next_ref, at_start, at_end, convw_ref, convb_ref,
                   wr_ref, wi_ref, rab_ref, ixb_ref, nsp_ref, out_ref,
                   xext_ref, a_ref, hs_ref, carry_ref, *, tt, reverse):
    bsz = main_ref.shape[0]
    rows = tt * SUBLANES
    halo_t = prev_ref.shape[1]
    halo = halo_t * SUBLANES
    width = main_ref.shape[2]
    n_lane_blk = width // LANES
    for b in range(bsz):
        xp = jnp.where(at_start, 0.0, prev_ref[b])
        xm = main_ref[b]
        xn = jnp.where(at_end, 0.0, next_ref[b])
        for l in range(n_lane_blk):
            lanes = slice(l * LANES, (l + 1) * LANES)
            xext_ref[l, _tm_rows(b, halo_t), :] = xp[:, lanes]
            xext_ref[l, _tm_rows(b, tt, halo), :] = xm[:, lanes]
            xext_ref[l, _tm_rows(b, halo_t, halo + rows), :] = xn[:, lanes]
    blk = LRU_HEADS_PER_BLOCK * (width // LRU_HEADS)
    lane_per_blk = blk // LANES
    for j in range(width // blk):
        cols = slice(j * blk, (j + 1) * blk)
        for r0 in range(0, rows, LRU_GATE_ROWS):
            parts = []
            for l in range(j * lane_per_blk, (j + 1) * lane_per_blk):
                lanes = slice(l * LANES, (l + 1) * LANES)
                acc = convb_ref[:, lanes]
                for tap in range(CONV_WIDTH):
                    first = halo + (tap - CONV_LEFT) * SUBLANES + r0
                    acc = acc + convw_ref[tap:tap + 1, lanes] * xext_ref[l, first:first + LRU_GATE_ROWS, :]
                parts.append(acc)
            xc = jnp.concatenate(parts, axis=-1)
            xcb = xc.astype(BF16)
            t_r = jnp.tanh(jnp.dot(xcb, wr_ref[j], preferred_element_type=F32) + rab_ref[:, cols])
            t_i = jnp.tanh(jnp.dot(xcb, wi_ref[j], preferred_element_type=F32) + ixb_ref[:, cols])
            log2_a = nsp_ref[:, cols] * (t_r + 1.0)
            a = jnp.exp2(log2_a)
            om = _quarter_one_minus_exp((2.0 * LN_2) * log2_a, log2_a, a)
            bt = (om * lax.rsqrt(jnp.maximum(om, SQRT_FLOOR))) * ((t_i + 1.0) * xc)
            for k in range(lane_per_blk):
                l = j * lane_per_blk + k
                a_ref[l, r0:r0 + LRU_GATE_ROWS, :] = a[:, k * LANES:(k + 1) * LANES]
                hs_ref[l, r0:r0 + LRU_GATE_ROWS, :] = bt[:, k * LANES:(k + 1) * LANES]

    for l in range(n_lane_blk):
        h_l = carry_ref[:, l * LANES:(l + 1) * LANES]
        for k in range(tt):
            t = (tt - 1 - k) if reverse else k
            r_t = slice(t * SUBLANES, (t + 1) * SUBLANES)
            h_l = a_ref[l, r_t, :] * h_l + hs_ref[l, r_t, :]
            hs_ref[l, r_t, :] = h_l
        carry_ref[:, l * LANES:(l + 1) * LANES] = h_l
    for b in range(bsz):
        out_ref[b] = jnp.concatenate(
            [hs_ref[l, _tm_rows(b, tt), :] for l in range(n_lane_blk)], axis=-1).astype(BF16)


def _lru_kernel(fm_ref, fp_ref, fn_ref, bm_ref, bp_ref, bn_ref, convw_ref, convb_ref,
                wr_ref, wi_ref, rab_ref, ixb_ref, nsp_ref, hf_ref, hb_ref,
                xext_ref, a_ref, hs_ref, carry_ref, *, tt):
    i = pl.program_id(0)
    last = pl.num_programs(0) - 1

    @pl.when(i == 0)
    def _():
        carry_ref[...] = jnp.zeros_like(carry_ref)

    _lru_direction(fm_ref, fp_ref, fn_ref, i == 0, i == last, convw_ref, convb_ref,
                   wr_ref.at[0], wi_ref.at[0], rab_ref.at[0], ixb_ref.at[0], nsp_ref.at[0], hf_ref,
                   xext_ref.at[0], a_ref.at[0], hs_ref.at[0], carry_ref.at[0], tt=tt, reverse=False)
    _lru_direction(bm_ref, bp_ref, bn_ref, i == last, i == 0, convw_ref, convb_ref,
                   wr_ref.at[1], wi_ref.at[1], rab_ref.at[1], ixb_ref.at[1], nsp_ref.at[1], hb_ref,
                   xext_ref.at[1], a_ref.at[1], hs_ref.at[1], carry_ref.at[1], tt=tt, reverse=True)


def _lru_scan(xr, conv_w, conv_b, wr, wi, ra_b, ix_b, nsp):
    bsz, seq, width = xr.shape
    tt = LRU_TIME_TILE
    rows = tt * bsz
    nc = seq // tt
    halo_t = SUBLANES
    per = tt // halo_t
    n_halo = seq // halo_t

    def main_f(i):
        return (0, i, 0)

    def prev_f(i):
        return (0, jnp.maximum(i * per - 1, 0), 0)

    def next_f(i):
        return (0, jnp.minimum((i + 1) * per, n_halo - 1), 0)

    def rev(f):
        return lambda i: f(nc - 1 - i)

    main_blk = (bsz, tt, width)
    halo_blk = (bsz, halo_t, width)
    kern = functools.partial(_lru_kernel, tt=tt)
    return pl.pallas_call(
        kern,
        grid=(nc,),
        in_specs=[
            pl.BlockSpec(main_blk, main_f),
            pl.BlockSpec(halo_blk, prev_f),
            pl.BlockSpec(halo_blk, next_f),
            pl.BlockSpec(main_blk, rev(main_f)),
            pl.BlockSpec(halo_blk, rev(prev_f)),
            pl.BlockSpec(halo_blk, rev(next_f)),
            _const_spec(conv_w.shape),
            _const_spec(conv_b.shape),
            _const_spec(wr.shape),
            _const_spec(wi.shape),
            _const_spec(ra_b.shape),
            _const_spec(ix_b.shape),
            _const_spec(nsp.shape),
        ],
        out_specs=[
            pl.BlockSpec(main_blk, main_f),
            pl.BlockSpec(main_blk, rev(main_f)),
        ],
        out_shape=[jax.ShapeDtypeStruct((bsz, seq, width), BF16)] * 2,
        scratch_shapes=[
            pltpu.VMEM((2, width // LANES, rows + 2 * halo_t * SUBLANES, LANES), F32),
            pltpu.VMEM((2, width // LANES, rows, LANES), F32),
            pltpu.VMEM((2, width // LANES, rows, LANES), F32),
            pltpu.VMEM((2, SUBLANES, width), F32),
        ],
        compiler_params=_cparams(("arbitrary",)),
        name="lru_scan",
    )(xr, xr, xr, xr, xr, xr, conv_w, conv_b, wr, wi, ra_b, ix_b, nsp)


def _rec_out_kernel(hf_ref, hb_ref, gate_ref, h_ref, mod_ref, wout_ref, nw2_ref, w1_ref, w2_ref,
                    fw_ref, o_ref, hmid_ref, *, final):
    mod = mod_ref[0]
    y = (hf_ref[0].astype(F32) + hb_ref[0].astype(F32)) * _gelu(gate_ref[0].astype(F32))
    mix = jnp.dot(y.astype(BF16), wout_ref[...], preferred_element_type=F32)
    _residual_mlp_tail(h_ref[0], mix, mod, nw2_ref, w1_ref, w2_ref, hmid_ref, fw_ref, o_ref, final)


def _rec_out(hf, hb, gate, h, mod, w_out, nw2, w1, w2, fw, final):
    bsz, seq, d = h.shape
    ts = TOKEN_TILE
    width = w_out.shape[0]
    d_ff = w1.shape[1]

    def tok(w):
        return pl.BlockSpec((1, ts, w), lambda b, s: (b, s, 0))

    kern = functools.partial(_rec_out_kernel, final=final)
    return pl.pallas_call(
        kern,
        grid=(bsz, seq // ts),
        in_specs=[
            tok(width), tok(width), tok(width), tok(d),
            pl.BlockSpec((1, 6, d), lambda b, s: (b, 0, 0)),
            _const_spec(w_out.shape),
            _const_spec((1, d)),
            _const_spec(w1.shape),
            _const_spec(w2.shape),
            _const_spec((1, d)),
        ],
        out_specs=tok(d),
        out_shape=jax.ShapeDtypeStruct((bsz, seq, d), F32),
        scratch_shapes=[pltpu.VMEM((ts, d_ff), BF16)],
        compiler_params=_cparams(("arbitrary", "arbitrary")),
        name="rec_out_mlp",
    )(hf, hb, gate, h, mod, w_out, nw2, w1, w2, fw)


def _s5_discretize(lam_re, lam_im, log_dt, b_re, b_im, c_re, c_im):
    lr = jnp.minimum(lam_re, S5_MAX_RE)
    li = lam_im
    dt = jnp.exp(log_dt)[..., None]
    mag = jnp.exp(lr * dt)
    abar_re = mag * jnp.cos(li * dt)
    abar_im = mag * jnp.sin(li * dt)
    den = lr * lr + li * li
    nr = abar_re - 1.0
    f_re = (nr * lr + abar_im * li) / den
    f_im = (abar_im * lr - nr * li) / den
    bb_re = f_re[..., None] * b_re - f_im[..., None] * b_im
    bb_im = f_re[..., None] * b_im + f_im[..., None] * b_re
    n_dir, groups, p, c = bb_re.shape
    gpb = S5_GROUPS_PER_BLOCK
    n_blk = groups // gpb
    eye = jnp.eye(gpb, dtype=F32)

    def pack_b(bb):
        bb = bb.reshape(n_dir, n_blk, gpb, p, c)
        return jnp.einsum("djgpc,gh->djgchp", bb, eye).reshape(n_dir, n_blk, gpb * c, gpb * p)

    def pack_c(cc):
        cc = cc.reshape(n_dir, n_blk, gpb, c, p)
        return jnp.einsum("djgcp,gh->djgphc", cc, eye).reshape(n_dir, n_blk, gpb * p, gpb * c)

    def pack_d(dd):
        dd = dd.reshape(n_dir, n_blk, gpb, c, c)
        return jnp.einsum("djgio,gh->djgiho", dd, eye).reshape(n_dir, n_blk, gpb * c, gpb * c)

    ar, ai = abar_re[..., None], abar_im[..., None]
    abb_re = ar * bb_re - ai * bb_im
    abb_im = ar * bb_im + ai * bb_re
    b_plain = jnp.concatenate([pack_b(bb_re), pack_b(bb_im)], axis=-1)
    b_adv = jnp.concatenate([pack_b(abb_re), pack_b(abb_im)], axis=-1)
    bpair = jnp.stack([jnp.concatenate([b_adv[0], b_plain[0]], axis=-2),
                       jnp.concatenate([b_plain[1], b_adv[1]], axis=-2)]).astype(BF16)

    cr, ci = abar_re[:, :, None, :], abar_im[:, :, None, :]
    ca_re = c_re * cr - c_im * ci
    ca_im = c_re * ci + c_im * cr
    c_plain = jnp.concatenate([pack_c(c_re), -pack_c(c_im)], axis=-2)
    c_adv = jnp.concatenate([pack_c(ca_re), -pack_c(ca_im)], axis=-2)
    cpair = jnp.stack([jnp.concatenate([c_plain[0], c_adv[0]], axis=-1),
                       jnp.concatenate([c_adv[1], c_plain[1]], axis=-1)]).astype(BF16)

    direct = pack_d(jnp.einsum("dgop,dgpi->dgio", c_re, bb_re) - jnp.einsum("dgop,dgpi->dgio", c_im, bb_im))
    zeros = jnp.zeros_like(direct[0])
    dmat = jnp.stack([jnp.concatenate([zeros, direct[0]], axis=-1),
                      jnp.concatenate([direct[1], zeros], axis=-1)]).astype(BF16)

    a2_re = (abar_re * abar_re - abar_im * abar_im).reshape(n_dir, n_blk, gpb * p)
    a2_im = (2.0 * abar_re * abar_im).reshape(n_dir, n_blk, gpb * p)
    return bpair, a2_re, a2_im, cpair, dmat


def _rope_tables(seq):
    rows = seq // GRID_W
    row_idx = jnp.repeat(jnp.arange(rows, dtype=F32), GRID_W)
    col_idx = jnp.tile(jnp.arange(GRID_W, dtype=F32), rows)
    inv_freq = ROPE_BASE ** (-jnp.arange(ROPE_FREQS, dtype=F32) / ROPE_FREQS)
    ang_r = row_idx[:, None] * inv_freq
    ang_c = col_idx[:, None] * inv_freq
    cos = jnp.concatenate([jnp.cos(ang_r)] * 2 + [jnp.cos(ang_c)] * 2, axis=-1)
    sin = jnp.concatenate([-jnp.sin(ang_r), jnp.sin(ang_r), -jnp.sin(ang_c), jnp.sin(ang_c)], axis=-1)
    reps = LANES // HEAD_DIM
    return jnp.tile(cos, (1, reps)), jnp.tile(sin, (1, reps))


def _block_diag_heads(w):
    n_dir, heads, i, j = w.shape
    hpb = LRU_HEADS_PER_BLOCK
    eye = jnp.eye(hpb, dtype=F32)
    w = w.reshape(n_dir, heads // hpb, hpb, i, j)
    return jnp.einsum("dnhij,hg->dnhigj", w, eye).reshape(n_dir, heads // hpb, hpb * i, hpb * j).astype(BF16)


def kernel(x, c, norm_w, ada_w, ada_b, mlp_w1, mlp_w2, final_norm_w, hyb_w_in, s5_lam_re, s5_lam_im,
           s5_log_dt, s5_b_re, s5_b_im, s5_c_re, s5_c_im, s5_d, s5_glu_w, s5_glu_b, attn_q_norm,
           attn_k_norm, hyb_w_out, rec_w_in, rec_conv_w, rec_conv_b, rec_ra_w, rec_ra_b, rec_ix_w,
           rec_ix_b, rec_lam, rec_w_out):
    bsz, seq, d = x.shape
    depth = norm_w.shape[0]
    assert bsz == SUBLANES, "the time-major recurrent layout needs one sublane tile of batch rows"
    s5_w = s5_d.shape[-1]
    kv_w = N_KV_HEADS * HEAD_DIM
    q_w = hyb_w_in.shape[-1] - s5_w - 2 * kv_w

    mod_all = _ada_mod(c, ada_w, ada_b).reshape(depth, bsz, 6, d)
    cos, sin = _rope_tables(seq)
    reps = LANES // HEAD_DIM
    head_id = jnp.arange(LANES) // HEAD_DIM
    g_ones = (head_id[:, None] == head_id[None, :]).astype(BF16)
    fw = final_norm_w.reshape(1, d)

    h = x
    for layer in range(depth):
        mod = mod_all[layer]
        nw1 = norm_w[layer, 0].reshape(1, d)
        nw2 = norm_w[layer, 1].reshape(1, d)
        w1 = mlp_w1[layer].astype(BF16)
        w2 = mlp_w2[layer].astype(BF16)
        final = layer == depth - 1
        if layer % 2 == 0:
            e = layer // 2
            qn_row = jnp.tile(attn_q_norm[e] * (HEAD_DIM ** -0.5 * LOG2_E), reps).reshape(1, LANES)
            kn_row = jnp.tile(attn_k_norm[e], reps).reshape(1, LANES)
            zs5, q, k, v = _hyb_in(h, mod, nw1, hyb_w_in[e].astype(BF16), g_ones, qn_row, kn_row,
                                   cos, sin, s5_w, q_w, kv_w)
            bpair, a2_re, a2_im, cpair, dmat = _s5_discretize(s5_lam_re[e], s5_lam_im[e], s5_log_dt[e],
                                                              s5_b_re[e], s5_b_im[e], s5_c_re[e], s5_c_im[e])
            yf, yb = _s5_scan(zs5, bpair, a2_re, a2_im, cpair, dmat)
            att = _attention(q, k, v)
            h = _hyb_out(zs5, yf, yb, att, h, mod, s5_d[e].reshape(1, s5_w), s5_glu_w[e].astype(BF16),
                         s5_glu_b[e].reshape(1, s5_w), hyb_w_out[e].astype(BF16), nw2, w1, w2, fw, final)
        else:
            o = layer // 2
            width = rec_w_out.shape[1]
            gate, xr = _rec_in(h, mod, nw1, rec_w_in[o].astype(BF16))
            nsp = ((-0.5 * LRU_C * LOG2_E) * jax.nn.softplus(-rec_lam[o])).reshape(2, 1, width)
            hf, hb = _lru_scan(xr, rec_conv_w[o], rec_conv_b[o].reshape(1, width),
                               _block_diag_heads(0.5 * rec_ra_w[o]), _block_diag_heads(0.5 * rec_ix_w[o]),
                               (0.5 * rec_ra_b[o]).reshape(2, 1, width), (0.5 * rec_ix_b[o]).reshape(2, 1, width),
                               nsp)
            h = _rec_out(hf, hb, gate, h, mod, rec_w_out[o].astype(BF16), nw2, w1, w2, fw, final)
    return h
```

```python
import functools
import math

import jax
import jax.numpy as jnp
from jax import lax
from jax.experimental import pallas as pl
from jax.experimental.pallas import tpu as pltpu

F32 = jnp.float32
BF16 = jnp.bfloat16

EPS = 1e-6
GRID_W = 64

S5_GROUP = 16
S5_STATE = 64
S5_MAX_RE = -1e-4
S5_GROUPS_PER_BLOCK = 8
S5_BLOCK_IN = S5_GROUPS_PER_BLOCK * S5_GROUP
S5_BLOCK_STATE = S5_GROUPS_PER_BLOCK * S5_STATE

HEAD_DIM = 64
N_KV_HEADS = 2
ROPE_FREQS = HEAD_DIM // 4
ROPE_BASE = 10000.0

LRU_HEADS = 16
LRU_HEADS_PER_BLOCK = 4
CONV_WIDTH = 4
CONV_LEFT = 2
LRU_C = 8.0

SUBLANES = 8
LANES = 128

TOKEN_TILE = 512
ATTN_Q_TILE = 512
ATTN_KV_TILE = 512
S5_TIME_TILE = 64
LRU_TIME_TILE = 64
LRU_GATE_ROWS = 128
FF_TILE = 512
VMEM_LIMIT = 56 * 1024 * 1024

SQRT_FLOOR = 1e-37
LOG2_E = math.log2(math.e)
LN_2 = math.log(2.0)


def _cparams(sem):
    return pltpu.CompilerParams(dimension_semantics=sem, vmem_limit_bytes=VMEM_LIMIT)


def _const_spec(shape):
    nd = len(shape)
    return pl.BlockSpec(shape, lambda *_: (0,) * nd, pipeline_mode=pl.Buffered(1))


def _norm_mod(x, w_row, sc_row, sh_row):
    rs = lax.rsqrt(jnp.mean(x * x, axis=-1, keepdims=True) + EPS)
    return x * rs * (w_row * (1.0 + sc_row)) + sh_row


def _sigmoid(x):
    return 1.0 / (1.0 + jnp.exp(-x))


def _gelu(x):
    return jax.nn.gelu(x, approximate=True)


def _lane_block_spec(width, ts):
    return pl.BlockSpec((1, width // LANES, ts, LANES), lambda b, s: (b, 0, s, 0))


def _cat_lane_blocks(ref):
    return jnp.concatenate([ref[0, j] for j in range(ref.shape[1])], axis=-1)


def _tm_rows(b, tt, first_row=0):
    return pl.ds(first_row + b, tt, stride=SUBLANES)


def _ada_kernel(c_ref, w_ref, b_ref, o_ref):
    c = c_ref[...]
    ca = (c * _sigmoid(c)).astype(BF16)
    o_ref[0] = jnp.dot(ca, w_ref[0].astype(BF16), preferred_element_type=F32) + b_ref[0]


def _ada_mod(c, ada_w, ada_b):
    depth, d, n = ada_w.shape
    bsz = c.shape[0]
    tn = 1024
    return pl.pallas_call(
        _ada_kernel,
        grid=(depth, n // tn),
        in_specs=[
            pl.BlockSpec((bsz, d), lambda l, j: (0, 0)),
            pl.BlockSpec((1, d, tn), lambda l, j: (l, 0, j)),
            pl.BlockSpec((1, 1, tn), lambda l, j: (l, 0, j)),
        ],
        out_specs=pl.BlockSpec((1, bsz, tn), lambda l, j: (l, 0, j)),
        out_shape=jax.ShapeDtypeStruct((depth, bsz, n), F32),
        compiler_params=_cparams(("arbitrary", "arbitrary")),
        name="ada_mod",
    )(c, ada_w, ada_b.reshape(depth, 1, n))


def _head_norm_rope(z, g_ones, nw_row, cos, sin_signed):
    ss = jnp.dot((z * z).astype(BF16), g_ones, preferred_element_type=F32)
    zn = z * lax.rsqrt(ss * (1.0 / HEAD_DIM) + EPS) * nw_row
    lane = lax.broadcasted_iota(jnp.int32, zn.shape, 1)
    first_half = (lane % (2 * ROPE_FREQS)) < ROPE_FREQS
    partner = jnp.where(first_half,
                        pltpu.roll(zn, LANES - ROPE_FREQS, axis=1),
                        pltpu.roll(zn, ROPE_FREQS, axis=1))
    return zn * cos + partner * sin_signed


def _hyb_in_kernel(h_ref, mod_ref, nw_ref, w_ref, gones_ref, qn_ref, kn_ref, cos_ref, sin_ref,
                   zs5_ref, q_ref, k_ref, v_ref, *, s5_w, q_w, kv_w):
    x = h_ref[0]
    mod = mod_ref[0]
    u = _norm_mod(x, nw_ref[...], mod[1:2], mod[0:1]).astype(BF16)
    z = jnp.dot(u, w_ref[...], preferred_element_type=F32)
    for j in range(s5_w // LANES):
        zs5_ref[0, j] = z[:, j * LANES:(j + 1) * LANES]
    cos = cos_ref[...]
    sin = sin_ref[...]
    g_ones = gones_ref[...]
    for j in range(q_w // LANES):
        zq = z[:, s5_w + j * LANES: s5_w + (j + 1) * LANES]
        q_ref[0, :, j * LANES:(j + 1) * LANES] = _head_norm_rope(
            zq, g_ones, qn_ref[...], cos, sin).astype(BF16)
    for j in range(kv_w // LANES):
        zk = z[:, s5_w + q_w + j * LANES: s5_w + q_w + (j + 1) * LANES]
        k_ref[0, :, j * LANES:(j + 1) * LANES] = _head_norm_rope(
            zk, g_ones, kn_ref[...], cos, sin).astype(BF16)
    v_ref[0] = z[:, s5_w + q_w + kv_w:].astype(BF16)


def _hyb_in(h, mod, nw, w_in, g_ones, qn_row, kn_row, cos, sin, s5_w, q_w, kv_w):
    bsz, seq, d = h.shape
    ts = TOKEN_TILE
    n = w_in.shape[1]
    kern = functools.partial(_hyb_in_kernel, s5_w=s5_w, q_w=q_w, kv_w=kv_w)

    def tok(width):
        return pl.BlockSpec((1, ts, width), lambda b, s: (b, s, 0))

    return pl.pallas_call(
        kern,
        grid=(bsz, seq // ts),
        in_specs=[
            tok(d),
            pl.BlockSpec((1, 6, d), lambda b, s: (b, 0, 0)),
            _const_spec((1, d)),
            _const_spec((d, n)),
            _const_spec((LANES, LANES)),
            _const_spec((1, LANES)),
            _const_spec((1, LANES)),
            pl.BlockSpec((ts, LANES), lambda b, s: (s, 0)),
            pl.BlockSpec((ts, LANES), lambda b, s: (s, 0)),
        ],
        out_specs=[_lane_block_spec(s5_w, ts), tok(q_w), tok(kv_w), tok(kv_w)],
        out_shape=[
            jax.ShapeDtypeStruct((bsz, s5_w // LANES, seq, LANES), F32),
            jax.ShapeDtypeStruct((bsz, seq, q_w), BF16),
            jax.ShapeDtypeStruct((bsz, seq, kv_w), BF16),
            jax.ShapeDtypeStruct((bsz, seq, kv_w), BF16),
        ],
        compiler_params=_cparams(("arbitrary", "arbitrary")),
        name="hyb_in",
    )(h, mod, nw, w_in, g_ones, qn_row, kn_row, cos, sin)


def _s5_kernel(u0_ref, u1_ref, bpair_ref, are_ref, aim_ref, cpair_ref, dmat_ref, y0_ref, y1_ref,
               tm_ref, ys_ref, bu_ref, carry_ref, *, tt, n_blk):
    bsz = u0_ref.shape[0]
    w = 2 * S5_BLOCK_STATE
    pairs = tt // 2
    prow = pairs * SUBLANES
    even, odd = 0, 1
    base = {(0, even): 0, (0, odd): SUBLANES, (1, even): 0, (1, odd): 0}

    @pl.when(pl.program_id(0) == 0)
    def _():
        carry_ref[...] = jnp.zeros_like(carry_ref)
        tm_ref[0, odd, :, 0:SUBLANES, :] = jnp.zeros((n_blk, SUBLANES, LANES), F32)
        tm_ref[1, even, :, prow:prow + SUBLANES, :] = jnp.zeros((n_blk, SUBLANES, LANES), F32)

    def to_time_major(d, u_ref):
        for b in range(bsz):
            for j in range(n_blk):
                for parity in (even, odd):
                    tm_ref[d, parity, j, _tm_rows(b, pairs, base[d, parity]), :] = (
                        u_ref[b, j, pl.ds(parity, pairs, stride=2), :])

    def pair_lhs(d, j):
        if d == 0:
            first, second = tm_ref[0, odd, j, 0:prow, :], tm_ref[0, even, j, 0:prow, :]
        else:
            first, second = tm_ref[1, odd, j, 0:prow, :], tm_ref[1, even, j, SUBLANES:SUBLANES + prow, :]
        return jnp.concatenate([first, second], axis=-1).astype(BF16)

    def project_in(d, j):
        bu_ref[d, :, j * w:(j + 1) * w] = jnp.dot(pair_lhs(d, j), bpair_ref[d, j], preferred_element_type=F32)

    def keep_neighbour_tile(d):
        for j in range(n_blk):
            if d == 0:
                tm_ref[0, odd, j, 0:SUBLANES, :] = tm_ref[0, odd, j, prow:prow + SUBLANES, :]
            else:
                tm_ref[1, even, j, prow:prow + SUBLANES, :] = tm_ref[1, even, j, 0:SUBLANES, :]

    def recurrence(d, j):
        re_cols = slice(j * w, j * w + S5_BLOCK_STATE)
        im_cols = slice(j * w + S5_BLOCK_STATE, (j + 1) * w)
        a_re = jnp.broadcast_to(are_ref[d, j:j + 1, :], (SUBLANES, S5_BLOCK_STATE))
        a_im = jnp.broadcast_to(aim_ref[d, j:j + 1, :], (SUBLANES, S5_BLOCK_STATE))
        h_re = carry_ref[d, :, re_cols]
        h_im = carry_ref[d, :, im_cols]
        for k in range(pairs):
            m = (pairs - 1 - k) if d else k
            rows = slice(m * SUBLANES, (m + 1) * SUBLANES)
            h_re, h_im = (a_re * h_re - a_im * h_im + bu_ref[d, rows, re_cols],
                          a_re * h_im + a_im * h_re + bu_ref[d, rows, im_cols])
            bu_ref[d, rows, re_cols] = h_re
            bu_ref[d, rows, im_cols] = h_im
        carry_ref[d, :, re_cols] = h_re
        carry_ref[d, :, im_cols] = h_im

    def project_out(d, j):
        direct_parity = odd if d == 0 else even
        first = base[d, direct_parity]
        u_direct = tm_ref[d, direct_parity, j, first:first + prow, :].astype(BF16)
        y = (jnp.dot(bu_ref[d, :, j * w:(j + 1) * w].astype(BF16), cpair_ref[d, j], preferred_element_type=F32)
             + jnp.dot(u_direct, dmat_ref[d, j], preferred_element_type=F32))
        ys_ref[d, even, j] = y[:, :LANES]
        ys_ref[d, odd, j] = y[:, LANES:]

    def to_batch_major(d, y_ref):
        for b in range(bsz):
            for j in range(n_blk):
                for parity in (even, odd):
                    y_ref[b, j, pl.ds(parity, pairs, stride=2), :] = ys_ref[d, parity, j, _tm_rows(b, pairs), :]

    to_time_major(0, u0_ref)
    to_time_major(1, u1_ref)
    for j in range(n_blk):
        project_in(0, j)
    for j in range(n_blk):
        recurrence(0, j)
        project_in(1, j)
    for j in range(n_blk):
        project_out(0, j)
        recurrence(1, j)
    keep_neighbour_tile(0)
    to_batch_major(0, y0_ref)
    for j in range(n_blk):
        project_out(1, j)
    keep_neighbour_tile(1)
    to_batch_major(1, y1_ref)


def _s5_scan(u, bpair, a_re, a_im, cpair, dmat):
    bsz, n_blk, seq, _ = u.shape
    tt = S5_TIME_TILE
    prow = (tt // 2) * bsz
    nc = seq // tt
    state_w = n_blk * 2 * S5_BLOCK_STATE
    kern = functools.partial(_s5_kernel, tt=tt, n_blk=n_blk)
    fwd = pl.BlockSpec((bsz, n_blk, tt, LANES), lambda i: (0, 0, i, 0))
    bwd = pl.BlockSpec((bsz, n_blk, tt, LANES), lambda i: (0, 0, nc - 1 - i, 0))
    return pl.pallas_call(
        kern,
        grid=(nc,),
        in_specs=[fwd, bwd, _const_spec(bpair.shape), _const_spec(a_re.shape), _const_spec(a_im.shape),
                  _const_spec(cpair.shape), _const_spec(dmat.shape)],
        out_specs=[fwd, bwd],
        out_shape=[jax.ShapeDtypeStruct(u.shape, F32)] * 2,
        scratch_shapes=[
            pltpu.VMEM((2, 2, n_blk, prow + SUBLANES, LANES), F32),
            pltpu.VMEM((2, 2, n_blk, prow, LANES), F32),
            pltpu.VMEM((2, prow, state_w), F32),
            pltpu.VMEM((2, SUBLANES, state_w), F32),
        ],
        compiler_params=_cparams(("arbitrary",)),
        name="s5_scan",
    )(u, u, bpair, a_re, a_im, cpair, dmat)


def _attn_kernel(q_ref, k_ref, v_ref, o_ref, kt_ref, vext_ref, s_ref, *, seq, tq, tk, n_heads):
    n_q = seq // tq
    n_kv = seq // tk
    heads_per_kv = n_heads // N_KV_HEADS

    def prep(c, carry):
        rows = pl.ds(pl.multiple_of(c * tk, tk), tk)
        kf = k_ref[0, rows, :].astype(F32)
        vf = v_ref[0, rows, :].astype(F32)
        lane = lax.broadcasted_iota(jnp.int32, kf.shape, 1)
        low = lane < HEAD_DIM
        k_sw = pltpu.roll(kf, HEAD_DIM, axis=1)
        v_sw = pltpu.roll(vf, HEAD_DIM, axis=1)
        kt_ref[0, 0, c] = jnp.where(low, kf, 0.0).T.astype(BF16)
        kt_ref[0, 1, c] = jnp.where(low, 0.0, k_sw).T.astype(BF16)
        kt_ref[1, 0, c] = jnp.where(low, k_sw, 0.0).T.astype(BF16)
        kt_ref[1, 1, c] = jnp.where(low, 0.0, kf).T.astype(BF16)
        vext_ref[0, 0, c] = jnp.where(low, vf, 1.0).astype(BF16)
        vext_ref[0, 1, c] = jnp.where(low, 1.0, v_sw).astype(BF16)
        vext_ref[1, 0, c] = jnp.where(low, v_sw, 1.0).astype(BF16)
        vext_ref[1, 1, c] = jnp.where(low, 1.0, vf).astype(BF16)
        return carry

    lax.fori_loop(0, n_kv, prep, 0)

    def head_ids(h):
        return h // heads_per_kv, h % 2, h // 2

    def scores_chunk(qp, h, c, m_part):
        kvh, e, _ = head_ids(h)
        s = jnp.dot(qp, kt_ref[kvh, e, c], preferred_element_type=F32)
        s_ref[h % 2, c] = s
        for j in range(tk // LANES):
            m_part = jnp.maximum(m_part, s[:, j * LANES:(j + 1) * LANES])
        return m_part

    def pv_chunk(h, c, m_full, acc):
        kvh, e, _ = head_ids(h)
        s = s_ref[h % 2, c]
        p = jnp.concatenate([jnp.exp2(s[:, j * LANES:(j + 1) * LANES] - m_full)
                             for j in range(tk // LANES)], axis=-1).astype(BF16)
        return acc + jnp.dot(p, vext_ref[kvh, e, c], preferred_element_type=F32)

    def row_max(m_part):
        return jnp.broadcast_to(jnp.max(m_part, axis=-1, keepdims=True), (tq, LANES))

    def q_pair(qt, h):
        pair = head_ids(h)[2]
        return q_ref[0, pl.ds(pl.multiple_of(qt * tq, tq), tq), pair * LANES:(pair + 1) * LANES]

    neg = jnp.full((tq, LANES), -jnp.inf, F32)
    zero = jnp.zeros((tq, LANES), F32)

    qp0 = q_pair(0, 0)
    m0 = row_max(lax.fori_loop(0, n_kv, lambda c, m: scores_chunk(qp0, 0, c, m), neg))

    def q_tile(qt, m_head0):
        m_cur = m_head0
        r_even = None
        for h in range(n_heads):
            h_next = (h + 1) % n_heads
            qt_next = qt if h_next else jnp.minimum(qt + 1, n_q - 1)
            qp = q_pair(qt_next, h_next)
            m_part, r = neg, zero
            for c in range(n_kv):
                m_part = scores_chunk(qp, h_next, c, m_part)
                r = pv_chunk(h, c, m_cur, r)
            m_cur = row_max(m_part)
            if h % 2 == 0:
                r_even = r
            else:
                pair = h // 2
                lane = lax.broadcasted_iota(jnp.int32, r.shape, 1)
                o_pair = jnp.where(lane < HEAD_DIM,
                                   r_even / pltpu.roll(r_even, HEAD_DIM, axis=1),
                                   r / pltpu.roll(r, HEAD_DIM, axis=1))
                o_ref[0, pl.ds(pl.multiple_of(qt * tq, tq), tq), pair * LANES:(pair + 1) * LANES] = (
                    o_pair.astype(BF16))
        return m_cur

    lax.fori_loop(0, n_q, q_tile, m0)


def _attention(q, k, v):
    bsz, seq, q_w = q.shape
    kv_w = k.shape[-1]
    tq, tk = ATTN_Q_TILE, ATTN_KV_TILE
    n_heads = q_w // HEAD_DIM
    kern = functools.partial(_attn_kernel, seq=seq, tq=tq, tk=tk, n_heads=n_heads)

    def row(width):
        return pl.BlockSpec((1, seq, width), lambda b: (b, 0, 0))

    return pl.pallas_call(
        kern,
        grid=(bsz,),
        in_specs=[row(q_w), row(kv_w), row(kv_w)],
        out_specs=row(q_w),
        out_shape=jax.ShapeDtypeStruct((bsz, seq, q_w), BF16),
        scratch_shapes=[
            pltpu.VMEM((N_KV_HEADS, 2, seq // tk, LANES, tk), BF16),
            pltpu.VMEM((N_KV_HEADS, 2, seq // tk, tk, LANES), BF16),
            pltpu.VMEM((2, seq // tk, tq, tk), F32),
        ],
        compiler_params=_cparams(("arbitrary",)),
        name="attention",
    )(q, k, v)


def _residual_mlp_tail(x, mix, mod, nw2_ref, w1_ref, w2_ref, hmid_ref, fw_ref, o_ref, final):
    h1 = x + mod[2:3] * mix
    u = _norm_mod(h1, nw2_ref[...], mod[4:5], mod[3:4]).astype(BF16)
    d_ff = w1_ref.shape[1]
    for f in range(d_ff // FF_TILE):
        cols = slice(f * FF_TILE, (f + 1) * FF_TILE)
        a = jnp.maximum(jnp.dot(u, w1_ref[:, cols], preferred_element_type=F32), 0.0)
        hmid_ref[:, cols] = (a * a).astype(BF16)
    ff = jnp.dot(hmid_ref[...], w2_ref[...], preferred_element_type=F32)
    h2 = h1 + mod[5:6] * ff
    if final:
        rs = lax.rsqrt(jnp.mean(h2 * h2, axis=-1, keepdims=True) + EPS)
        h2 = h2 * rs * fw_ref[...]
    o_ref[0] = h2


def _hyb_out_kernel(zs5_ref, yf_ref, yb_ref, att_ref, h_ref, mod_ref, d_ref, gluw_ref, glub_ref,
                    wout_ref, nw2_ref, w1_ref, w2_ref, fw_ref, o_ref, hmid_ref, *, s5_w, final):
    mod = mod_ref[0]
    y = _cat_lane_blocks(zs5_ref) * d_ref[...] + _cat_lane_blocks(yf_ref) + _cat_lane_blocks(yb_ref)
    g = _gelu(y)
    gate = _sigmoid(jnp.dot(g.astype(BF16), gluw_ref[...], preferred_element_type=F32) + glub_ref[...])
    ys5 = (g * gate).astype(BF16)
    mix = (jnp.dot(ys5, wout_ref[:s5_w, :], preferred_element_type=F32)
           + jnp.dot(att_ref[0], wout_ref[s5_w:, :], preferred_element_type=F32))
    _residual_mlp_tail(h_ref[0], mix, mod, nw2_ref, w1_ref, w2_ref, hmid_ref, fw_ref, o_ref, final)


def _hyb_out(zs5, yf, yb, att, h, mod, d_row, glu_w, glu_b, w_out, nw2, w1, w2, fw, final):
    bsz, seq, d = h.shape
    ts = TOKEN_TILE
    s5_w = glu_w.shape[0]
    d_ff = w1.shape[1]

    def tok(width):
        return pl.BlockSpec((1, ts, width), lambda b, s: (b, s, 0))

    kern = functools.partial(_hyb_out_kernel, s5_w=s5_w, final=final)
    return pl.pallas_call(
        kern,
        grid=(bsz, seq // ts),
        in_specs=[
            _lane_block_spec(s5_w, ts), _lane_block_spec(s5_w, ts), _lane_block_spec(s5_w, ts),
            tok(att.shape[-1]), tok(d),
            pl.BlockSpec((1, 6, d), lambda b, s: (b, 0, 0)),
            _const_spec((1, s5_w)),
            _const_spec(glu_w.shape),
            _const_spec((1, s5_w)),
            _const_spec(w_out.shape),
            _const_spec((1, d)),
            _const_spec(w1.shape),
            _const_spec(w2.shape),
            _const_spec((1, d)),
        ],
        out_specs=tok(d),
        out_shape=jax.ShapeDtypeStruct((bsz, seq, d), F32),
        scratch_shapes=[pltpu.VMEM((ts, d_ff), BF16)],
        compiler_params=_cparams(("arbitrary", "arbitrary")),
        name="hyb_out_mlp",
    )(zs5, yf, yb, att, h, mod, d_row, glu_w, glu_b, w_out, nw2, w1, w2, fw)


def _rec_in_kernel(h_ref, mod_ref, nw_ref, w_ref, gate_ref, xr_ref, *, width):
    mod = mod_ref[0]
    u = _norm_mod(h_ref[0], nw_ref[...], mod[1:2], mod[0:1]).astype(BF16)
    gate_ref[0] = jnp.dot(u, w_ref[:, :width], preferred_element_type=F32).astype(BF16)
    xr_ref[0] = jnp.dot(u, w_ref[:, width:], preferred_element_type=F32)


def _rec_in(h, mod, nw, w_in):
    bsz, seq, d = h.shape
    ts = TOKEN_TILE
    width = w_in.shape[1] // 2
    kern = functools.partial(_rec_in_kernel, width=width)

    def tok(w):
        return pl.BlockSpec((1, ts, w), lambda b, s: (b, s, 0))

    return pl.pallas_call(
        kern,
        grid=(bsz, seq // ts),
        in_specs=[
            tok(d),
            pl.BlockSpec((1, 6, d), lambda b, s: (b, 0, 0)),
            _const_spec((1, d)),
            _const_spec(w_in.shape),
        ],
        out_specs=[tok(width), tok(width)],
        out_shape=[
            jax.ShapeDtypeStruct((bsz, seq, width), BF16),
            jax.ShapeDtypeStruct((bsz, seq, width), F32),
        ],
        compiler_params=_cparams(("arbitrary", "arbitrary")),
        name="rec_in",
    )(h, mod, nw, w_in)


def _quarter_one_minus_exp(x, log2_a, a):
    series = (-0.5 * LN_2) * log2_a * (1.0 + x * (1 / 2 + x * (1 / 6)))
    return jnp.where(x > -1 / 32, series, 0.25 - 0.25 * (a * a))


def _lru_direction(main_ref, prev_ref, next_ref, at_start, at_end, convw_ref, convb_ref,
                   wr_ref, wi_ref, rab_ref, ixb_ref, nsp_ref, out_ref,
                   xext_ref, a_ref, hs_ref, carry_ref, *, tt, reverse):
    bsz = main_ref.shape[0]
    rows = tt * SUBLANES
    halo_t = prev_ref.shape[1]
    halo = halo_t * SUBLANES
    width = main_ref.shape[2]
    n_lane_blk = width // LANES
    for b in range(bsz):
        xp = jnp.where(at_start, 0.0, prev_ref[b])
        xm = main_ref[b]
        xn = jnp.where(at_end, 0.0, next_ref[b])
        for l in range(n_lane_blk):
            lanes = slice(l * LANES, (l + 1) * LANES)
            xext_ref[l, _tm_rows(b, halo_t), :] = xp[:, lanes]
            xext_ref[l, _tm_rows(b, tt, halo), :] = xm[:, lanes]
            xext_ref[l, _tm_rows(b, halo_t, halo + rows), :] = xn[:, lanes]
    blk = LRU_HEADS_PER_BLOCK * (width // LRU_HEADS)
    lane_per_blk = blk // LANES
    for j in range(width // blk):
        cols = slice(j * blk, (j + 1) * blk)
        for r0 in range(0, rows, LRU_GATE_ROWS):
            parts = []
            for l in range(j * lane_per_blk, (j + 1) * lane_per_blk):
                lanes = slice(l * LANES, (l + 1) * LANES)
                acc = convb_ref[:, lanes]
                for tap in range(CONV_WIDTH):
                    first = halo + (tap - CONV_LEFT) * SUBLANES + r0
                    acc = acc + convw_ref[tap:tap + 1, lanes] * xext_ref[l, first:first + LRU_GATE_ROWS, :]
                parts.append(acc)
            xc = jnp.concatenate(parts, axis=-1)
            xcb = xc.astype(BF16)
            t_r = jnp.tanh(jnp.dot(xcb, wr_ref[j], preferred_element_type=F32) + rab_ref[:, cols])
            t_i = jnp.tanh(jnp.dot(xcb, wi_ref[j], preferred_element_type=F32) + ixb_ref[:, cols])
            log2_a = nsp_ref[:, cols] * (t_r + 1.0)
            a = jnp.exp2(log2_a)
            om = _quarter_one_minus_exp((2.0 * LN_2) * log2_a, log2_a, a)
            bt = (om * lax.rsqrt(jnp.maximum(om, SQRT_FLOOR))) * ((t_i + 1.0) * xc)
            for k in range(lane_per_blk):
                l = j * lane_per_blk + k
                a_ref[l, r0:r0 + LRU_GATE_ROWS, :] = a[:, k * LANES:(k + 1) * LANES]
                hs_ref[l, r0:r0 + LRU_GATE_ROWS, :] = bt[:, k * LANES:(k + 1) * LANES]

    for l in range(n_lane_blk):
        h_l = carry_ref[:, l * LANES:(l + 1) * LANES]
        for k in range(tt):
            t = (tt - 1 - k) if reverse else k
            r_t = slice(t * SUBLANES, (t + 1) * SUBLANES)
            h_l = a_ref[l, r_t, :] * h_l + hs_ref[l, r_t, :]
            hs_ref[l, r_t, :] = h_l
        carry_ref[:, l * LANES:(l + 1) * LANES] = h_l
    for b in range(bsz):
        out_ref[b] = jnp.concatenate(
            [hs_ref[l, _tm_rows(b, tt), :] for l in range(n_lane_blk)], axis=-1).astype(BF16)


def _lru_kernel(fm_ref, fp_ref, fn_ref, bm_ref, bp_ref, bn_ref, convw_ref, convb_ref,
                wr_ref, wi_ref, rab_ref, ixb_ref, nsp_ref, hf_ref, hb_ref,
                xext_ref, a_ref, hs_ref, carry_ref, *, tt):
    i = pl.program_id(0)
    last = pl.num_programs(0) - 1

    @pl.when(i == 0)
    def _():
        carry_ref[...] = jnp.zeros_like(carry_ref)

    _lru_direction(fm_ref, fp_ref, fn_ref, i == 0, i == last, convw_ref, convb_ref,
                   wr_ref.at[0], wi_ref.at[0], rab_ref.at[0], ixb_ref.at[0], nsp_ref.at[0], hf_ref,
                   xext_ref.at[0], a_ref.at[0], hs_ref.at[0], carry_ref.at[0], tt=tt, reverse=False)
    _lru_direction(bm_ref, bp_ref, bn_ref, i == last, i == 0, convw_ref, convb_ref,
                   wr_ref.at[1], wi_ref.at[1], rab_ref.at[1], ixb_ref.at[1], nsp_ref.at[1], hb_ref,
                   xext_ref.at[1], a_ref.at[1], hs_ref.at[1], carry_ref.at[1], tt=tt, reverse=True)


def _lru_scan(xr, conv_w, conv_b, wr, wi, ra_b, ix_b, nsp):
    bsz, seq, width = xr.shape
    tt = LRU_TIME_TILE
    rows = tt * bsz
    nc = seq // tt
    halo_t = SUBLANES
    per = tt // halo_t
    n_halo = seq // halo_t

    def main_f(i):
        return (0, i, 0)

    def prev_f(i):
        return (0, jnp.maximum(i * per - 1, 0), 0)

    def next_f(i):
        return (0, jnp.minimum((i + 1) * per, n_halo - 1), 0)

    def rev(f):
        return lambda i: f(nc - 1 - i)

    main_blk = (bsz, tt, width)
    halo_blk = (bsz, halo_t, width)
    kern = functools.partial(_lru_kernel, tt=tt)
    return pl.pallas_call(
        kern,
        grid=(nc,),
        in_specs=[
            pl.BlockSpec(main_blk, main_f),
            pl.BlockSpec(halo_blk, prev_f),
            pl.BlockSpec(halo_blk, next_f),
            pl.BlockSpec(main_blk, rev(main_f)),
            pl.BlockSpec(halo_blk, rev(prev_f)),
            pl.BlockSpec(halo_blk, rev(next_f)),
            _const_spec(conv_w.shape),
            _const_spec(conv_b.shape),
            _const_spec(wr.shape),
            _const_spec(wi.shape),
            _const_spec(ra_b.shape),
            _const_spec(ix_b.shape),
            _const_spec(nsp.shape),
        ],
        out_specs=[
            pl.BlockSpec(main_blk, main_f),
            pl.BlockSpec(main_blk, rev(main_f)),
        ],
        out_shape=[jax.ShapeDtypeStruct((bsz, seq, width), BF16)] * 2,
        scratch_shapes=[
            pltpu.VMEM((2, width // LANES, rows + 2 * halo_t * SUBLANES, LANES), F32),
            pltpu.VMEM((2, width // LANES, rows, LANES), F32),
            pltpu.VMEM((2, width // LANES, rows, LANES), F32),
            pltpu.VMEM((2, SUBLANES, width), F32),
        ],
        compiler_params=_cparams(("arbitrary",)),
        name="lru_scan",
    )(xr, xr, xr, xr, xr, xr, conv_w, conv_b, wr, wi, ra_b, ix_b, nsp)


def _rec_out_kernel(hf_ref, hb_ref, gate_ref, h_ref, mod_ref, wout_ref, nw2_ref, w1_ref, w2_ref,
                    fw_ref, o_ref, hmid_ref, *, final):
    mod = mod_ref[0]
    y = (hf_ref[0].astype(F32) + hb_ref[0].astype(F32)) * _gelu(gate_ref[0].astype(F32))
    mix = jnp.dot(y.astype(BF16), wout_ref[...], preferred_element_type=F32)
    _residual_mlp_tail(h_ref[0], mix, mod, nw2_ref, w1_ref, w2_ref, hmid_ref, fw_ref, o_ref, final)


def _rec_out(hf, hb, gate, h, mod, w_out, nw2, w1, w2, fw, final):
    bsz, seq, d = h.shape
    ts = TOKEN_TILE
    width = w_out.shape[0]
    d_ff = w1.shape[1]

    def tok(w):
        return pl.BlockSpec((1, ts, w), lambda b, s: (b, s, 0))

    kern = functools.partial(_rec_out_kernel, final=final)
    return pl.pallas_call(
        kern,
        grid=(bsz, seq // ts),
        in_specs=[
            tok(width), tok(width), tok(width), tok(d),
            pl.BlockSpec((1, 6, d), lambda b, s: (b, 0, 0)),
            _const_spec(w_out.shape),
            _const_spec((1, d)),
            _const_spec(w1.shape),
            _const_spec(w2.shape),
            _const_spec((1, d)),
        ],
        out_specs=tok(d),
        out_shape=jax.ShapeDtypeStruct((bsz, seq, d), F32),
        scratch_shapes=[pltpu.VMEM((ts, d_ff), BF16)],
        compiler_params=_cparams(("arbitrary", "arbitrary")),
        name="rec_out_mlp",
    )(hf, hb, gate, h, mod, w_out, nw2, w1, w2, fw)


def _s5_discretize(lam_re, lam_im, log_dt, b_re, b_im, c_re, c_im):
    lr = jnp.minimum(lam_re, S5_MAX_RE)
    li = lam_im
    dt = jnp.exp(log_dt)[..., None]
    mag = jnp.exp(lr * dt)
    abar_re = mag * jnp.cos(li * dt)
    abar_im = mag * jnp.sin(li * dt)
    den = lr * lr + li * li
    nr = abar_re - 1.0
    f_re = (nr * lr + abar_im * li) / den
    f_im = (abar_im * lr - nr * li) / den
    bb_re = f_re[..., None] * b_re - f_im[..., None] * b_im
    bb_im = f_re[..., None] * b_im + f_im[..., None] * b_re
    n_dir, groups, p, c = bb_re.shape
    gpb = S5_GROUPS_PER_BLOCK
    n_blk = groups // gpb
    eye = jnp.eye(gpb, dtype=F32)

    def pack_b(bb):
        bb = bb.reshape(n_dir, n_blk, gpb, p, c)
        return jnp.einsum("djgpc,gh->djgchp", bb, eye).reshape(n_dir, n_blk, gpb * c, gpb * p)

    def pack_c(cc):
        cc = cc.reshape(n_dir, n_blk, gpb, c, p)
        return jnp.einsum("djgcp,gh->djgphc", cc, eye).reshape(n_dir, n_blk, gpb * p, gpb * c)

    def pack_d(dd):
        dd = dd.reshape(n_dir, n_blk, gpb, c, c)
        return jnp.einsum("djgio,gh->djgiho", dd, eye).reshape(n_dir, n_blk, gpb * c, gpb * c)

    ar, ai = abar_re[..., None], abar_im[..., None]
    abb_re = ar * bb_re - ai * bb_im
    abb_im = ar * bb_im + ai * bb_re
    b_plain = jnp.concatenate([pack_b(bb_re), pack_b(bb_im)], axis=-1)
    b_adv = jnp.concatenate([pack_b(abb_re), pack_b(abb_im)], axis=-1)
    bpair = jnp.stack([jnp.concatenate([b_adv[0], b_plain[0]], axis=-2),
                       jnp.concatenate([b_plain[1], b_adv[1]], axis=-2)]).astype(BF16)

    cr, ci = abar_re[:, :, None, :], abar_im[:, :, None, :]
    ca_re = c_re * cr - c_im * ci
    ca_im = c_re * ci + c_im * cr
    c_plain = jnp.concatenate([pack_c(c_re), -pack_c(c_im)], axis=-2)
    c_adv = jnp.concatenate([pack_c(ca_re), -pack_c(ca_im)], axis=-2)
    cpair = jnp.stack([jnp.concatenate([c_plain[0], c_adv[0]], axis=-1),
                       jnp.concatenate([c_adv[1], c_plain[1]], axis=-1)]).astype(BF16)

    direct = pack_d(jnp.einsum("dgop,dgpi->dgio", c_re, bb_re) - jnp.einsum("dgop,dgpi->dgio", c_im, bb_im))
    zeros = jnp.zeros_like(direct[0])
    dmat = jnp.stack([jnp.concatenate([zeros, direct[0]], axis=-1),
                      jnp.concatenate([direct[1], zeros], axis=-1)]).astype(BF16)

    a2_re = (abar_re * abar_re - abar_im * abar_im).reshape(n_dir, n_blk, gpb * p)
    a2_im = (2.0 * abar_re * abar_im).reshape(n_dir, n_blk, gpb * p)
    return bpair, a2_re, a2_im, cpair, dmat


def _rope_tables(seq):
    rows = seq // GRID_W
    row_idx = jnp.repeat(jnp.arange(rows, dtype=F32), GRID_W)
    col_idx = jnp.tile(jnp.arange(GRID_W, dtype=F32), rows)
    inv_freq = ROPE_BASE ** (-jnp.arange(ROPE_FREQS, dtype=F32) / ROPE_FREQS)
    ang_r = row_idx[:, None] * inv_freq
    ang_c = col_idx[:, None] * inv_freq
    cos = jnp.concatenate([jnp.cos(ang_r)] * 2 + [jnp.cos(ang_c)] * 2, axis=-1)
    sin = jnp.concatenate([-jnp.sin(ang_r), jnp.sin(ang_r), -jnp.sin(ang_c), jnp.sin(ang_c)], axis=-1)
    reps = LANES // HEAD_DIM
    return jnp.tile(cos, (1, reps)), jnp.tile(sin, (1, reps))


def _block_diag_heads(w):
    n_dir, heads, i, j = w.shape
    hpb = LRU_HEADS_PER_BLOCK
    eye = jnp.eye(hpb, dtype=F32)
    w = w.reshape(n_dir, heads // hpb, hpb, i, j)
    return jnp.einsum("dnhij,hg->dnhigj", w, eye).reshape(n_dir, heads // hpb, hpb * i, hpb * j).astype(BF16)


def kernel(x, c, norm_w, ada_w, ada_b, mlp_w1, mlp_w2, final_norm_w, hyb_w_in, s5_lam_re, s5_lam_im,
           s5_log_dt, s5_b_re, s5_b_im, s5_c_re, s5_c_im, s5_d, s5_glu_w, s5_glu_b, attn_q_norm,
           attn_k_norm, hyb_w_out, rec_w_in, rec_conv_w, rec_conv_b, rec_ra_w, rec_ra_b, rec_ix_w,
           rec_ix_b, rec_lam, rec_w_out):
    bsz, seq, d = x.shape
    depth = norm_w.shape[0]
    assert bsz == SUBLANES, "the time-major recurrent layout needs one sublane tile of batch rows"
    s5_w = s5_d.shape[-1]
    kv_w = N_KV_HEADS * HEAD_DIM
    q_w = hyb_w_in.shape[-1] - s5_w - 2 * kv_w

    mod_all = _ada_mod(c, ada_w, ada_b).reshape(depth, bsz, 6, d)
    cos, sin = _rope_tables(seq)
    reps = LANES // HEAD_DIM
    head_id = jnp.arange(LANES) // HEAD_DIM
    g_ones = (head_id[:, None] == head_id[None, :]).astype(BF16)
    fw = final_norm_w.reshape(1, d)

    h = x
    for layer in range(depth):
        mod = mod_all[layer]
        nw1 = norm_w[layer, 0].reshape(1, d)
        nw2 = norm_w[layer, 1].reshape(1, d)
        w1 = mlp_w1[layer].astype(BF16)
        w2 = mlp_w2[layer].astype(BF16)
        final = layer == depth - 1
        if layer % 2 == 0:
            e = layer // 2
            qn_row = jnp.tile(attn_q_norm[e] * (HEAD_DIM ** -0.5 * LOG2_E), reps).reshape(1, LANES)
            kn_row = jnp.tile(attn_k_norm[e], reps).reshape(1, LANES)
            zs5, q, k, v = _hyb_in(h, mod, nw1, hyb_w_in[e].astype(BF16), g_ones, qn_row, kn_row,
                                   cos, sin, s5_w, q_w, kv_w)
            bpair, a2_re, a2_im, cpair, dmat = _s5_discretize(s5_lam_re[e], s5_lam_im[e], s5_log_dt[e],
                                                              s5_b_re[e], s5_b_im[e], s5_c_re[e], s5_c_im[e])
            yf, yb = _s5_scan(zs5, bpair, a2_re, a2_im, cpair, dmat)
            att = _attention(q, k, v)
            h = _hyb_out(zs5, yf, yb, att, h, mod, s5_d[e].reshape(1, s5_w), s5_glu_w[e].astype(BF16),
                         s5_glu_b[e].reshape(1, s5_w), hyb_w_out[e].astype(BF16), nw2, w1, w2, fw, final)
        else:
            o = layer // 2
            width = rec_w_out.shape[1]
            gate, xr = _rec_in(h, mod, nw1, rec_w_in[o].astype(BF16))
            nsp = ((-0.5 * LRU_C * LOG2_E) * jax.nn.softplus(-rec_lam[o])).reshape(2, 1, width)
            hf, hb = _lru_scan(xr, rec_conv_w[o], rec_conv_b[o].reshape(1, width),
                               _block_diag_heads(0.5 * rec_ra_w[o]), _block_diag_heads(0.5 * rec_ix_w[o]),
                               (0.5 * rec_ra_b[o]).reshape(2, 1, width), (0.5 * rec_ix_b[o]).reshape(2, 1, width),
                               nsp)
            h = _rec_out(hf, hb, gate, h, mod, rec_w_out[o].astype(BF16), nw2, w1, w2, fw, final)
    return h
```
